```python
import math
import jax, jax.numpy as jnp
from jax import lax
import numpy as np

D_MODEL = 4096
BATCH = 1
SEQ = 16384
DEPTH = 4

N_EVEN = (DEPTH + 1) // 2
N_ODD = DEPTH // 2
N_MOD = 6
CONV_CH = D_MODEL // 2
CONV_K = 3
ATTN_HEAD_DIM = 128
ATTN_HEADS = (D_MODEL // 2) // (2 * ATTN_HEAD_DIM)
ATTN_WIDTH = ATTN_HEADS * 2 * ATTN_HEAD_DIM
MIX_WIDTH = CONV_CH + ATTN_WIDTH
IN_PROJ_WIDTH = 3 * CONV_CH + 3 * ATTN_WIDTH
Q_BLOCK = 128
S5_GROUP = 16
S5_STATE = 64
S5_GROUPS = D_MODEL // S5_GROUP
S5_CHUNK = 128
DT_MIN = 0.001
DT_MAX = 0.1
FFN_HIDDEN = -(-8 * D_MODEL // (3 * 256)) * 256
NORM_EPS = 1e-6
SUBLN_EPS = 1e-5

kernel_name = 'hybrid_conv_diffattn_s5_adaln_trunk'


def rmsnorm(x, g, eps=NORM_EPS):
    xf = x.astype(jnp.float32)
    y = xf * lax.rsqrt(jnp.mean(xf * xf, axis=-1, keepdims=True) + eps)
    return (y * g.astype(jnp.float32)).astype(x.dtype)


def modulate(h, shift, scale):
    return h * (1 + scale[:, None, :]) + shift[:, None, :]


def causal_depthwise_conv(u, w):
    k_width, ch = w.shape
    return lax.conv_general_dilated(
        u, w[:, None, :].astype(u.dtype), window_strides=(1,),
        padding=[(k_width - 1, 0)], dimension_numbers=('NWC', 'WIO', 'NWC'),
        feature_group_count=ch)


def diff_attention(q, k, v, lam, lam_init, subln_g):
    bsz, seq = q.shape[0], q.shape[1]
    nb = seq // Q_BLOCK
    q = q * jnp.asarray(ATTN_HEAD_DIM ** -0.5, q.dtype)
    q_blocks = jnp.moveaxis(q.reshape(bsz, nb, Q_BLOCK, *q.shape[2:]), 1, 0)
    key_pos = jnp.arange(seq)

    def block(args):
        q_blk, blk = args
        s = jnp.einsum('bqhmd,bkhmd->bhmqk', q_blk, k).astype(jnp.float32)
        q_pos = blk * Q_BLOCK + jnp.arange(Q_BLOCK)
        s = jnp.where(key_pos[None, :] <= q_pos[:, None], s, -jnp.inf)
        p = jax.nn.softmax(s, axis=-1)
        w = p[:, :, 0] - lam * p[:, :, 1]
        return jnp.einsum('bhqk,bkhe->bqhe', w.astype(v.dtype), v)

    o = lax.map(block, (q_blocks, jnp.arange(nb)))
    o = jnp.moveaxis(o, 0, 1).reshape(bsz, seq, ATTN_HEADS, 2 * ATTN_HEAD_DIM)
    o = rmsnorm(o, subln_g, SUBLN_EPS) * (1.0 - lam_init)
    return o.reshape(bsz, seq, ATTN_WIDTH)


def conv_diffattn_mixer(h, w_in, conv_w, lq1, lk1, lq2, lk2, subln_g, w_out, lam_init):
    bsz, seq, _ = h.shape
    z = h @ w_in
    splits = [CONV_CH, 2 * CONV_CH, 3 * CONV_CH,
              3 * CONV_CH + ATTN_WIDTH, 3 * CONV_CH + 2 * ATTN_WIDTH]
    gate_b, gate_c, x_in, q, k, v = jnp.split(z, splits, axis=-1)
    y_conv = gate_b * causal_depthwise_conv(gate_c * x_in, conv_w)
    f32 = jnp.float32
    lam = (jnp.exp(jnp.sum(lq1.astype(f32) * lk1.astype(f32)))
           - jnp.exp(jnp.sum(lq2.astype(f32) * lk2.astype(f32))) + lam_init)
    q = q.reshape(bsz, seq, ATTN_HEADS, 2, ATTN_HEAD_DIM)
    k = k.reshape(bsz, seq, ATTN_HEADS, 2, ATTN_HEAD_DIM)
    v = v.reshape(bsz, seq, ATTN_HEADS, 2 * ATTN_HEAD_DIM)
    y_attn = diff_attention(q, k, v, lam, lam_init, subln_g)
    return jnp.concatenate([y_conv, y_attn.astype(y_conv.dtype)], axis=-1) @ w_out


def s5_ssm(u, a_re, a_im, log_dt, b_re, b_im, c_re, c_im, d_skip):
    f32 = jnp.float32
    bsz, seq, _ = u.shape
    lam_re = jnp.minimum(a_re.astype(f32), -1e-4)
    lam_im = a_im.astype(f32)
    dt = jnp.exp(log_dt.astype(f32))[:, None]
    mag = jnp.exp(lam_re * dt)
    ab_re = mag * jnp.cos(lam_im * dt)
    ab_im = mag * jnp.sin(lam_im * dt)
    den = lam_re * lam_re + lam_im * lam_im
    nr, ni = ab_re - 1.0, ab_im
    f_re = (nr * lam_re + ni * lam_im) / den
    f_im = (ni * lam_re - nr * lam_im) / den
    b_re, b_im = b_re.astype(f32), b_im.astype(f32)
    bb_re = f_re[..., None] * b_re - f_im[..., None] * b_im
    bb_im = f_re[..., None] * b_im + f_im[..., None] * b_re
    c_re, c_im = c_re.astype(f32), c_im.astype(f32)

    uf = u.astype(f32)
    nc = seq // S5_CHUNK
    ug = jnp.moveaxis(uf.reshape(bsz, nc, S5_CHUNK, S5_GROUPS, S5_GROUP), 1, 0)
    a_shape = (bsz, S5_CHUNK, S5_GROUPS, S5_STATE)
    ar = jnp.broadcast_to(ab_re, a_shape)
    ai = jnp.broadcast_to(ab_im, a_shape)

    def combine(e1, e2):
        a1r, a1i, b1r, b1i = e1
        a2r, a2i, b2r, b2i = e2
        return (a2r * a1r - a2i * a1i, a2r * a1i + a2i * a1r,
                a2r * b1r - a2i * b1i + b2r, a2r * b1i + a2i * b1r + b2i)

    def step(carry, u_blk):
        h_re, h_im = carry
        bu_re = jnp.einsum('btgh,gph->btgp', u_blk, bb_re)
        bu_im = jnp.einsum('btgh,gph->btgp', u_blk, bb_im)
        acr, aci, sr, si = lax.associative_scan(combine, (ar, ai, bu_re, bu_im), axis=1)
        sr = sr + acr * h_re[:, None] - aci * h_im[:, None]
        si = si + acr * h_im[:, None] + aci * h_re[:, None]
        y = (jnp.einsum('btgp,gop->btgo', sr, c_re)
             - jnp.einsum('btgp,gop->btgo', si, c_im))
        return (sr[:, -1], si[:, -1]), y

    zeros = jnp.zeros((bsz, S5_GROUPS, S5_STATE), f32)
    _, ys = lax.scan(step, (zeros, zeros), ug)
    y = jnp.moveaxis(ys, 0, 1).reshape(bsz, seq, D_MODEL)
    return (y + d_skip.astype(f32) * uf).astype(u.dtype)


def s5_glu_mixer(h, a_re, a_im, log_dt, b_re, b_im, c_re, c_im, d_skip, glu_w1, glu_w2):
    g = jax.nn.gelu(s5_ssm(h, a_re, a_im, log_dt, b_re, b_im, c_re, c_im, d_skip))
    return (g @ glu_w1) * jax.nn.sigmoid(g @ glu_w2)


def swiglu(h, wg, wu, wd):
    return (jax.nn.silu(h @ wg) * (h @ wu)) @ wd


def setup_inputs(seed: int = 0) -> dict:
    key = jax.random.key(seed)
    ks = jax.random.split(key, 32)
    f32 = jnp.float32

    def nrm(k, shape, s):
        return jax.random.normal(k, shape, f32) * s

    n_idx = jnp.arange(S5_STATE, dtype=f32)
    return {
        'x': nrm(ks[0], (BATCH, SEQ, D_MODEL), 1.0),
        'c': nrm(ks[1], (BATCH, D_MODEL), 1.0),
        'w_ada': nrm(ks[2], (D_MODEL, N_MOD * D_MODEL), 0.5 * D_MODEL ** -0.5),
        'b_ada': nrm(ks[3], (N_MOD * D_MODEL,), 0.02),
        'ada_table': nrm(ks[4], (DEPTH, N_MOD, D_MODEL), 0.1),
        'norm_mix': 1.0 + nrm(ks[5], (DEPTH, D_MODEL), 0.02),
        'norm_ffn': 1.0 + nrm(ks[6], (DEPTH, D_MODEL), 0.02),
        'norm_final': 1.0 + nrm(ks[7], (D_MODEL,), 0.02),
        'mix_w_in': nrm(ks[8], (N_EVEN, D_MODEL, IN_PROJ_WIDTH), D_MODEL ** -0.5),
        'conv_w': nrm(ks[9], (N_EVEN, CONV_K, CONV_CH), CONV_K ** -0.5),
        'lambda_q1': nrm(ks[10], (N_EVEN, ATTN_HEAD_DIM), 0.1),
        'lambda_k1': nrm(ks[11], (N_EVEN, ATTN_HEAD_DIM), 0.1),
        'lambda_q2': nrm(ks[12], (N_EVEN, ATTN_HEAD_DIM), 0.1),
        'lambda_k2': nrm(ks[13], (N_EVEN, ATTN_HEAD_DIM), 0.1),
        'subln_g': 1.0 + nrm(ks[14], (N_EVEN, 2 * ATTN_HEAD_DIM), 0.02),
        'mix_w_out': nrm(ks[15], (N_EVEN, MIX_WIDTH, D_MODEL), MIX_WIDTH ** -0.5),
        's5_a_re': -0.5 + nrm(ks[16], (N_ODD, S5_GROUPS, S5_STATE), 0.01),
        's5_a_im': math.pi * n_idx + nrm(ks[17], (N_ODD, S5_GROUPS, S5_STATE), 0.01),
        's5_log_dt': jax.random.uniform(ks[18], (N_ODD, S5_GROUPS), f32,
                                        math.log(DT_MIN), math.log(DT_MAX)),
        's5_b_re': nrm(ks[19], (N_ODD, S5_GROUPS, S5_STATE, S5_GROUP), (2 * S5_GROUP) ** -0.5),
        's5_b_im': nrm(ks[20], (N_ODD, S5_GROUPS, S5_STATE, S5_GROUP), (2 * S5_GROUP) ** -0.5),
        's5_c_re': nrm(ks[21], (N_ODD, S5_GROUPS, S5_GROUP, S5_STATE), S5_STATE ** -0.5),
        's5_c_im': nrm(ks[22], (N_ODD, S5_GROUPS, S5_GROUP, S5_STATE), S5_STATE ** -0.5),
        's5_d': nrm(ks[23], (N_ODD, D_MODEL), 1.0),
        'glu_w1': nrm(ks[24], (N_ODD, D_MODEL, D_MODEL), D_MODEL ** -0.5),
        'glu_w2': nrm(ks[25], (N_ODD, D_MODEL, D_MODEL), D_MODEL ** -0.5),
        'ffn_w_gate': nrm(ks[26], (DEPTH, D_MODEL, FFN_HIDDEN), D_MODEL ** -0.5),
        'ffn_w_up': nrm(ks[27], (DEPTH, D_MODEL, FFN_HIDDEN), D_MODEL ** -0.5),
        'ffn_w_down': nrm(ks[28], (DEPTH, FFN_HIDDEN, D_MODEL), FFN_HIDDEN ** -0.5),
    }


def reference(x, c, w_ada, b_ada, ada_table, norm_mix, norm_ffn, norm_final,
              mix_w_in, conv_w, lambda_q1, lambda_k1, lambda_q2, lambda_k2, subln_g, mix_w_out,
              s5_a_re, s5_a_im, s5_log_dt, s5_b_re, s5_b_im, s5_c_re, s5_c_im, s5_d,
              glu_w1, glu_w2, ffn_w_gate, ffn_w_up, ffn_w_down):
    dtype = x.dtype
    mod = (jax.nn.silu(c) @ w_ada + b_ada).reshape(c.shape[0], N_MOD, D_MODEL)
    for l in range(DEPTH):
        m = mod + ada_table[l][None]
        sh_m, sc_m, g_m, sh_f, sc_f, g_f = (m[:, 0], m[:, 1], m[:, 2],
                                            m[:, 3], m[:, 4], m[:, 5])
        h = modulate(rmsnorm(x, norm_mix[l]), sh_m, sc_m)
        if l % 2 == 0:
            e = l // 2
            lam_init = 0.8 - 0.6 * math.exp(-0.3 * l)
            y = conv_diffattn_mixer(h, mix_w_in[e], conv_w[e], lambda_q1[e], lambda_k1[e],
                                    lambda_q2[e], lambda_k2[e], subln_g[e], mix_w_out[e],
                                    lam_init)
        else:
            o = l // 2
            y = s5_glu_mixer(h, s5_a_re[o], s5_a_im[o], s5_log_dt[o], s5_b_re[o], s5_b_im[o],
                             s5_c_re[o], s5_c_im[o], s5_d[o], glu_w1[o], glu_w2[o])
        x = x + (g_m[:, None, :] * y).astype(dtype)
        h = modulate(rmsnorm(x, norm_ffn[l]), sh_f, sc_f)
        f = swiglu(h, ffn_w_gate[l], ffn_w_up[l], ffn_w_down[l])
        x = x + (g_f[:, None, :] * f).astype(dtype)
    return rmsnorm(x, norm_final)
```

```python
import functools
import math

import jax
import jax.numpy as jnp
from jax import lax
from jax.experimental import pallas as pl
from jax.experimental.pallas import tpu as pltpu

F32 = jnp.float32
BF16 = jnp.bfloat16

N_MOD = 6
CONV_K = 3
ATTN_HEAD_DIM = 128
S5_GROUP = 16
S5_STATE = 64
NORM_EPS = 1e-6
SUBLN_EPS = 1e-5

GROUPS_PER_BLOCK = 16
CH_BLOCK = GROUPS_PER_BLOCK * S5_GROUP
ST_BLOCK = GROUPS_PER_BLOCK * S5_STATE
SUBLANES = 8
LANES = 128
MASK_VALUE = -1e30
VMEM_LIMIT = 56 * 1024 * 1024


def _params(*sem):
    return pltpu.CompilerParams(dimension_semantics=sem, vmem_limit_bytes=VMEM_LIMIT)


def _tile(dim, pref):
    t = min(pref, dim)
    while dim % t:
        t //= 2
    return t


def _round_up(n, m):
    return -(-n // m) * m


def _ada_kernel(c_ref, w_ref, b_ref, t_ref, o_ref):
    c = c_ref[...]
    a = (c * jax.nn.sigmoid(c)).astype(BF16)
    r = jnp.dot(a, w_ref[...].astype(BF16), preferred_element_type=F32)
    o_ref[...] = r[0:1, :] + b_ref[...] + t_ref[...]


def _ada(c, w_ada, b_ada, ada_table):
    d = c.shape[1]
    depth = ada_table.shape[0]
    n = w_ada.shape[1]
    tn = _tile(n, 512)
    c8 = jnp.broadcast_to(c, (SUBLANES, d))
    out = pl.pallas_call(
        _ada_kernel,
        grid=(n // tn,),
        in_specs=[pl.BlockSpec((SUBLANES, d), lambda j: (0, 0)),
                  pl.BlockSpec((d, tn), lambda j: (0, j)),
                  pl.BlockSpec((1, tn), lambda j: (0, j)),
                  pl.BlockSpec((depth, tn), lambda j: (0, j))],
        out_specs=pl.BlockSpec((depth, tn), lambda j: (0, j)),
        out_shape=jax.ShapeDtypeStruct((depth, n), F32),
        compiler_params=_params("arbitrary"),
        name="ada_proj",
    )(c8, w_ada, b_ada.reshape(1, n), ada_table.reshape(depth, n))
    return out.reshape(depth * N_MOD, d)


def _norm_mod_kernel(x_ref, g_ref, m_ref, o_ref, *, shift_row, scale_row):
    x = x_ref[...]
    ms = jnp.mean(x * x, axis=-1, keepdims=True)
    y = x * lax.rsqrt(ms + NORM_EPS) * g_ref[...]
    shift = m_ref[shift_row:shift_row + 1, :]
    scale = m_ref[scale_row:scale_row + 1, :]
    o_ref[...] = (y * (1.0 + scale) + shift).astype(o_ref.dtype)


def _norm_mod(x, g, mods, shift_row, scale_row):
    l, d = x.shape
    tm = _tile(l, 512)
    return pl.pallas_call(
        functools.partial(_norm_mod_kernel, shift_row=shift_row, scale_row=scale_row),
        grid=(l // tm,),
        in_specs=[pl.BlockSpec((tm, d), lambda i: (i, 0)),
                  pl.BlockSpec((1, d), lambda i: (0, 0)),
                  pl.BlockSpec(mods.shape, lambda i: (0, 0))],
        out_specs=pl.BlockSpec((tm, d), lambda i: (i, 0)),
        out_shape=jax.ShapeDtypeStruct((l, d), BF16),
        compiler_params=_params("arbitrary"),
        name="norm_mod",
    )(x, g.reshape(1, d), mods)


def _norm_kernel(x_ref, g_ref, o_ref):
    x = x_ref[...]
    ms = jnp.mean(x * x, axis=-1, keepdims=True)
    o_ref[...] = x * lax.rsqrt(ms + NORM_EPS) * g_ref[...]


def _final_norm(x, g):
    l, d = x.shape
    tm = _tile(l, 512)
    return pl.pallas_call(
        _norm_kernel,
        grid=(l // tm,),
        in_specs=[pl.BlockSpec((tm, d), lambda i: (i, 0)),
                  pl.BlockSpec((1, d), lambda i: (0, 0))],
        out_specs=pl.BlockSpec((tm, d), lambda i: (i, 0)),
        out_shape=jax.ShapeDtypeStruct((l, d), F32),
        compiler_params=_params("arbitrary"),
        name="final_norm",
    )(x, g.reshape(1, d))


def _mm_kernel(a_ref, w_ref, o_ref):
    o_ref[...] = jnp.dot(a_ref[...], w_ref[...], preferred_element_type=F32).astype(o_ref.dtype)


def _matmul(a, w):
    m, k = a.shape
    n = w.shape[1]
    tm, tn = _tile(m, 1024), _tile(n, 1024)
    return pl.pallas_call(
        _mm_kernel,
        grid=(m // tm, n // tn),
        in_specs=[pl.BlockSpec((tm, k), lambda i, j: (i, 0)),
                  pl.BlockSpec((k, tn), lambda i, j: (0, j))],
        out_specs=pl.BlockSpec((tm, tn), lambda i, j: (i, j)),
        out_shape=jax.ShapeDtypeStruct((m, n), BF16),
        compiler_params=_params("arbitrary", "arbitrary"),
        name="in_proj",
    )(a, w)


def _swiglu_up_kernel(a_ref, wg_ref, wu_ref, o_ref):
    a = a_ref[...]
    g = jnp.dot(a, wg_ref[...], preferred_element_type=F32)
    u = jnp.dot(a, wu_ref[...], preferred_element_type=F32)
    o_ref[...] = (g * jax.nn.sigmoid(g) * u).astype(o_ref.dtype)


def _swiglu_up(a, wg, wu):
    m, k = a.shape
    n = wg.shape[1]
    tm, tn = _tile(m, 1024), _tile(n, 512)
    return pl.pallas_call(
        _swiglu_up_kernel,
        grid=(m // tm, n // tn),
        in_specs=[pl.BlockSpec((tm, k), lambda i, j: (i, 0)),
                  pl.BlockSpec((k, tn), lambda i, j: (0, j)),
                  pl.BlockSpec((k, tn), lambda i, j: (0, j))],
        out_specs=pl.BlockSpec((tm, tn), lambda i, j: (i, j)),
        out_shape=jax.ShapeDtypeStruct((m, n), BF16),
        compiler_params=_params("arbitrary", "arbitrary"),
        name="ffn_up",
    )(a, wg, wu)


def _glu_res_kernel(a_ref, w1_ref, w2_ref, x_ref, m_ref, o_ref, *, gate_row):
    a = a_ref[...]
    y1 = jnp.dot(a, w1_ref[...], preferred_element_type=F32)
    y2 = jnp.dot(a, w2_ref[...], preferred_element_type=F32)
    gate = m_ref[gate_row:gate_row + 1, :]
    o_ref[...] = x_ref[...] + gate * (y1 * jax.nn.sigmoid(y2))


def _glu_res(a, w1, w2, x, mods, gate_row):
    m, k = a.shape
    n = w1.shape[1]
    tm, tn = _tile(m, 1024), _tile(n, 512)
    return pl.pallas_call(
        functools.partial(_glu_res_kernel, gate_row=gate_row),
        grid=(m // tm, n // tn),
        in_specs=[pl.BlockSpec((tm, k), lambda i, j: (i, 0)),
                  pl.BlockSpec((k, tn), lambda i, j: (0, j)),
                  pl.BlockSpec((k, tn), lambda i, j: (0, j)),
                  pl.BlockSpec((tm, tn), lambda i, j: (i, j)),
                  pl.BlockSpec((mods.shape[0], tn), lambda i, j: (0, j))],
        out_specs=pl.BlockSpec((tm, tn), lambda i, j: (i, j)),
        out_shape=jax.ShapeDtypeStruct((m, n), F32),
        compiler_params=_params("arbitrary", "arbitrary"),
        name="glu_res",
    )(a, w1, w2, x, mods)


def _mm_res_kernel(a_ref, w_ref, x_ref, m_ref, o_ref, acc_ref, *, gate_row, nk):
    kk = pl.program_id(2)
    part = jnp.dot(a_ref[...], w_ref[...], preferred_element_type=F32)

    @pl.when(kk == 0)
    def _():
        acc_ref[...] = part

    @pl.when(kk > 0)
    def _():
        acc_ref[...] += part

    @pl.when(kk == nk - 1)
    def _():
        gate = m_ref[gate_row:gate_row + 1, :]
        o_ref[...] = x_ref[...] + gate * acc_ref[...]


def _mm_res(a, w, x, mods, gate_row, tk):
    m, k = a.shape
    n = w.shape[1]
    tm, tn = _tile(m, 1024), _tile(n, 1024)
    nk = k // tk
    return pl.pallas_call(
        functools.partial(_mm_res_kernel, gate_row=gate_row, nk=nk),
        grid=(m // tm, n // tn, nk),
        in_specs=[pl.BlockSpec((tm, tk), lambda i, j, kk: (i, kk)),
                  pl.BlockSpec((tk, tn), lambda i, j, kk: (kk, j)),
                  pl.BlockSpec((tm, tn), lambda i, j, kk: (i, j)),
                  pl.BlockSpec((mods.shape[0], tn), lambda i, j, kk: (0, j))],
        out_specs=pl.BlockSpec((tm, tn), lambda i, j, kk: (i, j)),
        out_shape=jax.ShapeDtypeStruct((m, n), F32),
        scratch_shapes=[pltpu.VMEM((tm, tn), F32)],
        compiler_params=_params("arbitrary", "arbitrary", "arbitrary"),
        name="ffn_down",
    )(a, w, x, mods)


def _mm2_res_kernel(a1_ref, a2_ref, w1_ref, w2_ref, x_ref, m_ref, o_ref, *, gate_row):
    y = jnp.dot(a1_ref[...], w1_ref[...], preferred_element_type=F32)
    y = y + jnp.dot(a2_ref[...], w2_ref[...], preferred_element_type=F32)
    gate = m_ref[gate_row:gate_row + 1, :]
    o_ref[...] = x_ref[...] + gate * y


def _mm2_res(a1, a2, w, x, mods, gate_row):
    m, k1 = a1.shape
    k2 = a2.shape[1]
    assert k1 == k2 and w.shape[0] == k1 + k2
    n = w.shape[1]
    tm, tn = _tile(m, 1024), _tile(n, 512)
    return pl.pallas_call(
        functools.partial(_mm2_res_kernel, gate_row=gate_row),
        grid=(m // tm, n // tn),
        in_specs=[pl.BlockSpec((tm, k1), lambda i, j: (i, 0)),
                  pl.BlockSpec((tm, k2), lambda i, j: (i, 0)),
                  pl.BlockSpec((k1, tn), lambda i, j: (0, j)),
                  pl.BlockSpec((k2, tn), lambda i, j: (1, j)),
                  pl.BlockSpec((tm, tn), lambda i, j: (i, j)),
                  pl.BlockSpec((mods.shape[0], tn), lambda i, j: (0, j))],
        out_specs=pl.BlockSpec((tm, tn), lambda i, j: (i, j)),
        out_shape=jax.ShapeDtypeStruct((m, n), F32),
        compiler_params=_params("arbitrary", "arbitrary"),
        name="out_proj",
    )(a1, a2, w, w, x, mods)


def _conv_kernel(b_ref, c_ref, xi_ref, ch_ref, xh_ref, w_ref, o_ref):
    i = pl.program_id(0)
    p = c_ref[...].astype(F32) * xi_ref[...].astype(F32)
    halo = ch_ref[...].astype(F32) * xh_ref[...].astype(F32)
    halo = halo * jnp.where(i > 0, 1.0, 0.0)
    nh = halo.shape[0]
    prev1 = halo[nh - 1:nh, :]
    prev2 = halo[nh - 2:nh - 1, :]
    row = lax.broadcasted_iota(jnp.int32, p.shape, 0)
    p1 = jnp.where(row == 0, prev1, pltpu.roll(p, 1, 0))
    p2 = jnp.where(row == 0, prev2, jnp.where(row == 1, prev1, pltpu.roll(p, 2, 0)))
    w = w_ref[...]
    y = w[2:3, :] * p + w[1:2, :] * p1 + w[0:1, :] * p2
    o_ref[...] = (b_ref[...].astype(F32) * y).astype(o_ref.dtype)


def _conv(z, conv_w):
    l = z.shape[0]
    ch = conv_w.shape[1]
    tm, tc = _tile(l, 512), _tile(ch, 512)
    halo = 16
    nc = ch // tc
    rb = tm // halo
    return pl.pallas_call(
        _conv_kernel,
        grid=(l // tm, nc),
        in_specs=[pl.BlockSpec((tm, tc), lambda i, j: (i, j)),
                  pl.BlockSpec((tm, tc), lambda i, j: (i, nc + j)),
                  pl.BlockSpec((tm, tc), lambda i, j: (i, 2 * nc + j)),
                  pl.BlockSpec((halo, tc), lambda i, j: (jnp.maximum(i * rb - 1, 0), nc + j)),
                  pl.BlockSpec((halo, tc), lambda i, j: (jnp.maximum(i * rb - 1, 0), 2 * nc + j)),
                  pl.BlockSpec((CONV_K, tc), lambda i, j: (0, j))],
        out_specs=pl.BlockSpec((tm, tc), lambda i, j: (i, j)),
        out_shape=jax.ShapeDtypeStruct((l, ch), BF16),
        compiler_params=_params("arbitrary", "arbitrary"),
        name="gated_conv",
    )(z, z, z, z, z, conv_w)


def _attn_kernel(lam_ref, g_ref, q_ref, k_ref, v_ref, o_ref,
                 m1_ref, l1_ref, a1_ref, m2_ref, l2_ref, a2_ref, *, tq, lam_init):
    qi = pl.program_id(1)
    d = ATTN_HEAD_DIM
    scale = d ** -0.5
    q = q_ref[...].astype(F32) * scale
    q1 = q[:, :d].astype(BF16)
    q2 = q[:, d:].astype(BF16)

    m1_ref[...] = jnp.full(m1_ref.shape, MASK_VALUE, F32)
    m2_ref[...] = jnp.full(m2_ref.shape, MASK_VALUE, F32)
    l1_ref[...] = jnp.zeros(l1_ref.shape, F32)
    l2_ref[...] = jnp.zeros(l2_ref.shape, F32)
    a1_ref[...] = jnp.zeros(a1_ref.shape, F32)
    a2_ref[...] = jnp.zeros(a2_ref.shape, F32)

    nt = (((1,), (1,)), ((), ()))

    def online(s, v, m_ref, l_ref, a_ref):
        m_old = m_ref[...]
        m_new = jnp.maximum(m_old, jnp.max(s, axis=-1, keepdims=True))
        alpha = jnp.exp(m_old - m_new)
        p = jnp.exp(s - m_new)
        l_ref[...] = alpha * l_ref[...] + jnp.sum(p, axis=-1, keepdims=True)
        a_ref[...] = alpha * a_ref[...] + jnp.dot(p.astype(BF16), v, preferred_element_type=F32)
        m_ref[...] = m_new

    def step(kb, masked):
        ks = pl.multiple_of(kb * tq, tq)
        k = k_ref[pl.ds(ks, tq), :]
        v = v_ref[pl.ds(ks, tq), :]
        s1 = lax.dot_general(q1, k[:, :d], nt, preferred_element_type=F32)
        s2 = lax.dot_general(q2, k[:, d:], nt, preferred_element_type=F32)
        if masked:
            keep = (lax.broadcasted_iota(jnp.int32, s1.shape, 1)
                    <= lax.broadcasted_iota(jnp.int32, s1.shape, 0))
            s1 = jnp.where(keep, s1, MASK_VALUE)
            s2 = jnp.where(keep, s2, MASK_VALUE)
        online(s1, v, m1_ref, l1_ref, a1_ref)
        online(s2, v, m2_ref, l2_ref, a2_ref)

    def body(kb, carry):
        step(kb, False)
        return carry

    lax.fori_loop(0, qi, body, 0)
    step(qi, True)

    lv = lam_ref[...]
    lam = (jnp.exp(jnp.sum(lv[0:1, :] * lv[1:2, :], axis=-1, keepdims=True))
           - jnp.exp(jnp.sum(lv[2:3, :] * lv[3:4, :], axis=-1, keepdims=True)) + lam_init)
    o = a1_ref[...] / l1_ref[...] - lam * (a2_ref[...] / l2_ref[...])
    ms = jnp.mean(o * o, axis=-1, keepdims=True)
    o = o * lax.rsqrt(ms + SUBLN_EPS) * g_ref[...] * (1.0 - lam_init)
    o_ref[...] = o.astype(o_ref.dtype)


def _attention(z, lam_vecs, subln_g, lam_init, conv_ch, attn_width):
    l = z.shape[0]
    hw = 2 * ATTN_HEAD_DIM
    heads = attn_width // hw
    tq = _tile(l, 512)
    qoff = 3 * conv_ch // hw
    koff = qoff + heads
    voff = koff + heads
    return pl.pallas_call(
        functools.partial(_attn_kernel, tq=tq, lam_init=lam_init),
        grid=(heads, l // tq),
        in_specs=[pl.BlockSpec((4, ATTN_HEAD_DIM), lambda h, i: (0, 0)),
                  pl.BlockSpec((1, hw), lambda h, i: (0, 0)),
                  pl.BlockSpec((tq, hw), lambda h, i: (i, qoff + h)),
                  pl.BlockSpec((l, hw), lambda h, i: (0, koff + h)),
                  pl.BlockSpec((l, hw), lambda h, i: (0, voff + h))],
        out_specs=pl.BlockSpec((tq, hw), lambda h, i: (i, h)),
        out_shape=jax.ShapeDtypeStruct((l, attn_width), BF16),
        scratch_shapes=[pltpu.VMEM((tq, 1), F32), pltpu.VMEM((tq, 1), F32), pltpu.VMEM((tq, hw), F32),
                        pltpu.VMEM((tq, 1), F32), pltpu.VMEM((tq, 1), F32), pltpu.VMEM((tq, hw), F32)],
        compiler_params=_params("arbitrary", "arbitrary"),
        name="diff_attn",
    )(lam_vecs, subln_g.reshape(1, hw), z, z, z)


def _s5_prep_kernel(are_ref, aim_ref, ldt_ref, bre_ref, bim_ref, bbre_ref, bbim_ref, pwre_ref, pwim_ref):
    lam_re = jnp.minimum(are_ref[...], -1e-4)
    lam_im = aim_ref[...]
    dt = jnp.exp(ldt_ref[...])
    mag = jnp.exp(lam_re * dt)
    ab_re = mag * jnp.cos(lam_im * dt)
    ab_im = mag * jnp.sin(lam_im * dt)
    den = lam_re * lam_re + lam_im * lam_im
    nr, ni = ab_re - 1.0, ab_im
    f_re = (nr * lam_re + ni * lam_im) / den
    f_im = (ni * lam_re - nr * lam_im) / den
    b_re, b_im = bre_ref[...], bim_ref[...]
    bbre_ref[...] = f_re * b_re - f_im * b_im
    bbim_ref[...] = f_re * b_im + f_im * b_re
    pr, pi = ab_re, ab_im
    pwre_ref[0] = pr
    pwim_ref[0] = pi
    for k in range(1, SUBLANES):
        pr, pi = pr * ab_re - pi * ab_im, pr * ab_im + pi * ab_re
        pwre_ref[k] = pr
        pwim_ref[k] = pi


def _s5_prepare(a_re, a_im, log_dt, b_re, b_im, c_re, c_im):
    g, p = a_re.shape
    h = S5_GROUP
    w = p * h
    rep = lambda t: jnp.repeat(t, h, axis=1)
    tg = _tile(g, 64)
    spec = pl.BlockSpec((tg, w), lambda i: (i, 0))
    spec3 = pl.BlockSpec((SUBLANES, tg, w), lambda i: (0, i, 0))
    bb_re, bb_im, pw_re, pw_im = pl.pallas_call(
        _s5_prep_kernel,
        grid=(g // tg,),
        in_specs=[spec] * 5,
        out_specs=[spec, spec, spec3, spec3],
        out_shape=[jax.ShapeDtypeStruct((g, w), F32)] * 2 + [jax.ShapeDtypeStruct((SUBLANES, g, w), F32)] * 2,
        compiler_params=_params("arbitrary"),
        name="s5_prep",
    )(rep(a_re), rep(a_im), jnp.broadcast_to(log_dt[:, None], (g, w)),
      b_re.reshape(g, w), b_im.reshape(g, w))

    nb = g // GROUPS_PER_BLOCK
    eye = jnp.eye(GROUPS_PER_BLOCK, dtype=F32)

    def in_mat(bb):
        t = bb.reshape(nb, GROUPS_PER_BLOCK, p, h).transpose(0, 1, 3, 2)
        t = t[:, :, :, None, :] * eye[None, :, None, :, None]
        return t.reshape(nb, CH_BLOCK, ST_BLOCK)

    def out_mat(c):
        t = c.reshape(nb, GROUPS_PER_BLOCK, h, p).transpose(0, 1, 3, 2)
        t = t[:, :, :, None, :] * eye[None, :, None, :, None]
        return t.reshape(nb, ST_BLOCK, CH_BLOCK)

    b_mat = jnp.concatenate([in_mat(bb_re), in_mat(bb_im)], axis=2).astype(BF16)
    c_mat = jnp.concatenate([out_mat(c_re), -out_mat(c_im)], axis=1).astype(BF16)

    def powers(pw):
        return pw[:, :, ::h].reshape(SUBLANES, nb, ST_BLOCK).transpose(1, 0, 2)

    pr, pi = powers(pw_re), powers(pw_im)
    t_idx = jnp.arange(SUBLANES)[None, :, None]
    tabs = []
    for dshift in (1, 2, 4):
        keep = (t_idx >= dshift).astype(F32)
        tabs += [pr[:, dshift - 1:dshift, :] * keep, pi[:, dshift - 1:dshift, :] * keep]
    tabs += [pr, pi]
    return b_mat, c_mat, jnp.stack(tabs, axis=1)


def _s5_kernel(u_ref, b_ref, c_ref, t_ref, d_ref, o_ref, st_ref, carry_ref, *, tt):
    ci = pl.program_id(1)

    @pl.when(ci == 0)
    def _():
        carry_ref[...] = jnp.zeros(carry_ref.shape, F32)

    u = u_ref[...]
    st_ref[...] = jnp.dot(u, b_ref[...], preferred_element_type=F32)

    for j in range(ST_BLOCK // LANES):
        lo = j * LANES
        hi = ST_BLOCK + lo
        tab = [t_ref[k, :, lo:lo + LANES] for k in range(8)]

        def tile(r, carry, lo=lo, hi=hi, tab=tab):
            cr, cim = carry
            row = pl.multiple_of(r * SUBLANES, SUBLANES)
            xr = st_ref[pl.ds(row, SUBLANES), lo:lo + LANES]
            xi = st_ref[pl.ds(row, SUBLANES), hi:hi + LANES]
            for n, dshift in enumerate((1, 2, 4)):
                ar, ai = tab[2 * n], tab[2 * n + 1]
                rr = pltpu.roll(xr, dshift, 0)
                ri = pltpu.roll(xi, dshift, 0)
                xr, xi = xr + ar * rr - ai * ri, xi + ar * ri + ai * rr
            pr, pi = tab[6], tab[7]
            xr, xi = xr + pr * cr - pi * cim, xi + pr * cim + pi * cr
            st_ref[pl.ds(row, SUBLANES), lo:lo + LANES] = xr
            st_ref[pl.ds(row, SUBLANES), hi:hi + LANES] = xi
            last = SUBLANES - 1
            return (jnp.broadcast_to(xr[last:last + 1, :], xr.shape),
                    jnp.broadcast_to(xi[last:last + 1, :], xi.shape))

        cr, cim = lax.fori_loop(0, tt // SUBLANES, tile,
                                (carry_ref[0, :, lo:lo + LANES], carry_ref[1, :, lo:lo + LANES]))
        carry_ref[0, :, lo:lo + LANES] = cr
        carry_ref[1, :, lo:lo + LANES] = cim

    y = jnp.dot(st_ref[...].astype(BF16), c_ref[...], preferred_element_type=F32)
    y = y + d_ref[...] * u.astype(F32)
    o_ref[...] = jax.nn.gelu(y).astype(o_ref.dtype)


def _s5(u, b_mat, c_mat, tabs, d_skip):
    l, d = u.shape
    nb = d // CH_BLOCK
    tt = _tile(l, 512)
    return pl.pallas_call(
        functools.partial(_s5_kernel, tt=tt),
        grid=(nb, l // tt),
        in_specs=[pl.BlockSpec((tt, CH_BLOCK), lambda b, i: (i, b)),
                  pl.BlockSpec((None, CH_BLOCK, 2 * ST_BLOCK), lambda b, i: (b, 0, 0)),
                  pl.BlockSpec((None, 2 * ST_BLOCK, CH_BLOCK), lambda b, i: (b, 0, 0)),
                  pl.BlockSpec((None, 8, SUBLANES, ST_BLOCK), lambda b, i: (b, 0, 0, 0)),
                  pl.BlockSpec((1, CH_BLOCK), lambda b, i: (0, b))],
        out_specs=pl.BlockSpec((tt, CH_BLOCK), lambda b, i: (i, b)),
        out_shape=jax.ShapeDtypeStruct((l, d), BF16),
        scratch_shapes=[pltpu.VMEM((tt, 2 * ST_BLOCK), F32),
                        pltpu.VMEM((2, SUBLANES, ST_BLOCK), F32)],
        compiler_params=_params("arbitrary", "arbitrary"),
        name="s5_scan",
    )(u, b_mat, c_mat, tabs, d_skip.reshape(1, d))


def kernel(x, c, w_ada, b_ada, ada_table, norm_mix, norm_ffn, norm_final, mix_w_in, conv_w, lambda_q1, lambda_k1, lambda_q2, lambda_k2, subln_g, mix_w_out, s5_a_re, s5_a_im, s5_log_dt, s5_b_re, s5_b_im, s5_c_re, s5_c_im, s5_d, glu_w1, glu_w2, ffn_w_gate, ffn_w_up, ffn_w_down):
    bsz, seq, d = x.shape
    assert bsz == 1
    depth = ada_table.shape[0]
    conv_ch = conv_w.shape[2]
    attn_width = mix_w_out.shape[1] - conv_ch
    hidden = ffn_w_gate.shape[2]
    hidden_pad = _round_up(hidden, 1024)
    tk_down = hidden_pad // (hidden_pad // 1024 if (hidden_pad // 1024) % 4 else 4)
    pad = hidden_pad - hidden

    mods = _ada(c, w_ada, b_ada, ada_table)
    xs = x.reshape(seq, d)
    for l in range(depth):
        r = l * N_MOD
        h = _norm_mod(xs, norm_mix[l], mods, r + 0, r + 1)
        if l % 2 == 0:
            e = l // 2
            lam_init = 0.8 - 0.6 * math.exp(-0.3 * l)
            z = _matmul(h, mix_w_in[e].astype(BF16))
            y_conv = _conv(z, conv_w[e])
            lam_vecs = jnp.stack([lambda_q1[e], lambda_k1[e], lambda_q2[e], lambda_k2[e]])
            y_attn = _attention(z, lam_vecs, subln_g[e], lam_init, conv_ch, attn_width)
            xs = _mm2_res(y_conv, y_attn, mix_w_out[e].astype(BF16), xs, mods, r + 2)
        else:
            o = l // 2
            b_mat, c_mat, tabs = _s5_prepare(s5_a_re[o], s5_a_im[o], s5_log_dt[o],
                                             s5_b_re[o], s5_b_im[o], s5_c_re[o], s5_c_im[o])
            g = _s5(h, b_mat, c_mat, tabs, s5_d[o])
            xs = _glu_res(g, glu_w1[o].astype(BF16), glu_w2[o].astype(BF16), xs, mods, r + 2)
        h = _norm_mod(xs, norm_ffn[l], mods, r + 3, r + 4)
        wg = jnp.pad(ffn_w_gate[l].astype(BF16), ((0, 0), (0, pad)))
        wu = jnp.pad(ffn_w_up[l].astype(BF16), ((0, 0), (0, pad)))
        wd = jnp.pad(ffn_w_down[l].astype(BF16), ((0, pad), (0, 0)))
        act = _swiglu_up(h, wg, wu)
        xs = _mm_res(act, wd, xs, mods, r + 5, tk_down)
    return _final_norm(xs, norm_final).reshape(bsz, seq, d)
```

```python
import functools
import math

import jax
import jax.numpy as jnp
from jax import lax
from jax.experimental import pallas as pl
from jax.experimental.pallas import tpu as pltpu

F32 = jnp.float32
BF16 = jnp.bfloat16

N_MOD = 6
CONV_K = 3
ATTN_HEAD_DIM = 128
S5_GROUP = 16
S5_STATE = 64
NORM_EPS = 1e-6
SUBLN_EPS = 1e-5

GROUPS_PER_BLOCK = 16
CH_BLOCK = GROUPS_PER_BLOCK * S5_GROUP
ST_BLOCK = GROUPS_PER_BLOCK * S5_STATE
SUBLANES = 8
LANES = 128
MASK_VALUE = -1e30
VMEM_LIMIT = 56 * 1024 * 1024


def _params(*sem):
    return pltpu.CompilerParams(dimension_semantics=sem, vmem_limit_bytes=VMEM_LIMIT)


def _tile(dim, pref):
    t = min(pref, dim)
    while dim % t:
        t //= 2
    return t


def _round_up(n, m):
    return -(-n // m) * m


def _ada_kernel(c_ref, w_ref, b_ref, t_ref, o_ref):
    c = c_ref[...]
    a = (c * jax.nn.sigmoid(c)).astype(BF16)
    r = jnp.dot(a, w_ref[...].astype(BF16), preferred_element_type=F32)
    o_ref[...] = r[0:1, :] + b_ref[...] + t_ref[...]


def _ada(c, w_ada, b_ada, ada_table):
    d = c.shape[1]
    depth = ada_table.shape[0]
    n = w_ada.shape[1]
    tn = _tile(n, 512)
    c8 = jnp.broadcast_to(c, (SUBLANES, d))
    out = pl.pallas_call(
        _ada_kernel,
        grid=(n // tn,),
        in_specs=[pl.BlockSpec((SUBLANES, d), lambda j: (0, 0)),
                  pl.BlockSpec((d, tn), lambda j: (0, j)),
                  pl.BlockSpec((1, tn), lambda j: (0, j)),
                  pl.BlockSpec((depth, tn), lambda j: (0, j))],
        out_specs=pl.BlockSpec((depth, tn), lambda j: (0, j)),
        out_shape=jax.ShapeDtypeStruct((depth, n), F32),
        compiler_params=_params("arbitrary"),
        name="ada_proj",
    )(c8, w_ada, b_ada.reshape(1, n), ada_table.reshape(depth, n))
    return out.reshape(depth * N_MOD, d)


def _norm_mod_kernel(x_ref, g_ref, m_ref, o_ref, *, shift_row, scale_row):
    x = x_ref[...]
    ms = jnp.mean(x * x, axis=-1, keepdims=True)
    y = x * lax.rsqrt(ms + NORM_EPS) * g_ref[...]
    shift = m_ref[shift_row:shift_row + 1, :]
    scale = m_ref[scale_row:scale_row + 1, :]
    o_ref[...] = (y * (1.0 + scale) + shift).astype(o_ref.dtype)


def _norm_mod(x, g, mods, shift_row, scale_row):
    l, d = x.shape
    tm = _tile(l, 512)
    return pl.pallas_call(
        functools.partial(_norm_mod_kernel, shift_row=shift_row, scale_row=scale_row),
        grid=(l // tm,),
        in_specs=[pl.BlockSpec((tm, d), lambda i: (i, 0)),
                  pl.BlockSpec((1, d), lambda i: (0, 0)),
                  pl.BlockSpec(mods.shape, lambda i: (0, 0))],
        out_specs=pl.BlockSpec((tm, d), lambda i: (i, 0)),
        out_shape=jax.ShapeDtypeStruct((l, d), BF16),
        compiler_params=_params("arbitrary"),
        name="norm_mod",
    )(x, g.reshape(1, d), mods)


def _norm_kernel(x_ref, g_ref, o_ref):
    x = x_ref[...]
    ms = jnp.mean(x * x, axis=-1, keepdims=True)
    o_ref[...] = x * lax.rsqrt(ms + NORM_EPS) * g_ref[...]


def _final_norm(x, g):
    l, d = x.shape
    tm = _tile(l, 512)
    return pl.pallas_call(
        _norm_kernel,
        grid=(l // tm,),
        in_specs=[pl.BlockSpec((tm, d), lambda i: (i, 0)),
                  pl.BlockSpec((1, d), lambda i: (0, 0))],
        out_specs=pl.BlockSpec((tm, d), lambda i: (i, 0)),
        out_shape=jax.ShapeDtypeStruct((l, d), F32),
        compiler_params=_params("arbitrary"),
        name="final_norm",
    )(x, g.reshape(1, d))


def _mm_kernel(a_ref, w_ref, o_ref):
    o_ref[...] = jnp.dot(a_ref[...], w_ref[...], preferred_element_type=F32).astype(o_ref.dtype)


def _matmul(a, w, layer):
    m, k = a.shape
    n = w.shape[2]
    tm, tn = _tile(m, 1024), _tile(n, 1024)
    return pl.pallas_call(
        _mm_kernel,
        grid=(m // tm, n // tn),
        in_specs=[pl.BlockSpec((tm, k), lambda i, j: (i, 0)),
                  pl.BlockSpec((None, k, tn), lambda i, j: (layer, 0, j))],
        out_specs=pl.BlockSpec((tm, tn), lambda i, j: (i, j)),
        out_shape=jax.ShapeDtypeStruct((m, n), BF16),
        compiler_params=_params("arbitrary", "arbitrary"),
        name="in_proj",
    )(a, w)


def _swiglu_up_kernel(a_ref, wg_ref, wu_ref, o_ref):
    a = a_ref[...]
    g = jnp.dot(a, wg_ref[...], preferred_element_type=F32)
    u = jnp.dot(a, wu_ref[...], preferred_element_type=F32)
    o_ref[...] = (g * jax.nn.sigmoid(g) * u).astype(o_ref.dtype)


def _swiglu_up(a, wg, wu, layer):
    m, k = a.shape
    n = wg.shape[2]
    tm, tn = _tile(m, 1024), _tile(n, 512)
    return pl.pallas_call(
        _swiglu_up_kernel,
        grid=(m // tm, n // tn),
        in_specs=[pl.BlockSpec((tm, k), lambda i, j: (i, 0)),
                  pl.BlockSpec((None, k, tn), lambda i, j: (layer, 0, j)),
                  pl.BlockSpec((None, k, tn), lambda i, j: (layer, 0, j))],
        out_specs=pl.BlockSpec((tm, tn), lambda i, j: (i, j)),
        out_shape=jax.ShapeDtypeStruct((m, n), BF16),
        compiler_params=_params("arbitrary", "arbitrary"),
        name="ffn_up",
    )(a, wg, wu)


def _glu_res_kernel(a_ref, w1_ref, w2_ref, x_ref, m_ref, o_ref, *, gate_row):
    a = a_ref[...]
    y1 = jnp.dot(a, w1_ref[...], preferred_element_type=F32)
    y2 = jnp.dot(a, w2_ref[...], preferred_element_type=F32)
    gate = m_ref[gate_row:gate_row + 1, :]
    o_ref[...] = x_ref[...] + gate * (y1 * jax.nn.sigmoid(y2))


def _glu_res(a, w1, w2, layer, x, mods, gate_row):
    m, k = a.shape
    n = w1.shape[2]
    tm, tn = _tile(m, 1024), _tile(n, 512)
    return pl.pallas_call(
        functools.partial(_glu_res_kernel, gate_row=gate_row),
        grid=(m // tm, n // tn),
        in_specs=[pl.BlockSpec((tm, k), lambda i, j: (i, 0)),
                  pl.BlockSpec((None, k, tn), lambda i, j: (layer, 0, j)),
                  pl.BlockSpec((None, k, tn), lambda i, j: (layer, 0, j)),
                  pl.BlockSpec((tm, tn), lambda i, j: (i, j)),
                  pl.BlockSpec((mods.shape[0], tn), lambda i, j: (0, j))],
        out_specs=pl.BlockSpec((tm, tn), lambda i, j: (i, j)),
        out_shape=jax.ShapeDtypeStruct((m, n), F32),
        compiler_params=_params("arbitrary", "arbitrary"),
        name="glu_res",
    )(a, w1, w2, x, mods)


def _mm_res_kernel(a_ref, w_ref, x_ref, m_ref, o_ref, acc_ref, *, gate_row, nk):
    kk = pl.program_id(2)
    part = jnp.dot(a_ref[...], w_ref[...], preferred_element_type=F32)

    @pl.when(kk == 0)
    def _():
        acc_ref[...] = part

    @pl.when(kk > 0)
    def _():
        acc_ref[...] += part

    @pl.when(kk == nk - 1)
    def _():
        gate = m_ref[gate_row:gate_row + 1, :]
        o_ref[...] = x_ref[...] + gate * acc_ref[...]


def _mm_res(a, w, layer, x, mods, gate_row, tk):
    m, k = a.shape
    n = w.shape[2]
    tm, tn = _tile(m, 1024), _tile(n, 1024)
    nk = k // tk
    return pl.pallas_call(
        functools.partial(_mm_res_kernel, gate_row=gate_row, nk=nk),
        grid=(m // tm, n // tn, nk),
        in_specs=[pl.BlockSpec((tm, tk), lambda i, j, kk: (i, kk)),
                  pl.BlockSpec((None, tk, tn), lambda i, j, kk: (layer, kk, j)),
                  pl.BlockSpec((tm, tn), lambda i, j, kk: (i, j)),
                  pl.BlockSpec((mods.shape[0], tn), lambda i, j, kk: (0, j))],
        out_specs=pl.BlockSpec((tm, tn), lambda i, j, kk: (i, j)),
        out_shape=jax.ShapeDtypeStruct((m, n), F32),
        scratch_shapes=[pltpu.VMEM((tm, tn), F32)],
        compiler_params=_params("arbitrary", "arbitrary", "arbitrary"),
        name="ffn_down",
    )(a, w, x, mods)


def _mm2_res_kernel(a1_ref, a2_ref, w1_ref, w2_ref, x_ref, m_ref, o_ref, *, gate_row):
    y = jnp.dot(a1_ref[...], w1_ref[...], preferred_element_type=F32)
    y = y + jnp.dot(a2_ref[...], w2_ref[...], preferred_element_type=F32)
    gate = m_ref[gate_row:gate_row + 1, :]
    o_ref[...] = x_ref[...] + gate * y


def _mm2_res(a1, a2, w, layer, x, mods, gate_row):
    m, k1 = a1.shape
    k2 = a2.shape[1]
    assert k1 == k2 and w.shape[1] == k1 + k2
    n = w.shape[2]
    tm, tn = _tile(m, 1024), _tile(n, 512)
    return pl.pallas_call(
        functools.partial(_mm2_res_kernel, gate_row=gate_row),
        grid=(m // tm, n // tn),
        in_specs=[pl.BlockSpec((tm, k1), lambda i, j: (i, 0)),
                  pl.BlockSpec((tm, k2), lambda i, j: (i, 0)),
                  pl.BlockSpec((None, k1, tn), lambda i, j: (layer, 0, j)),
                  pl.BlockSpec((None, k2, tn), lambda i, j: (layer, 1, j)),
                  pl.BlockSpec((tm, tn), lambda i, j: (i, j)),
                  pl.BlockSpec((mods.shape[0], tn), lambda i, j: (0, j))],
        out_specs=pl.BlockSpec((tm, tn), lambda i, j: (i, j)),
        out_shape=jax.ShapeDtypeStruct((m, n), F32),
        compiler_params=_params("arbitrary", "arbitrary"),
        name="out_proj",
    )(a1, a2, w, w, x, mods)


def _conv_kernel(b_ref, c_ref, xi_ref, ch_ref, xh_ref, w_ref, o_ref):
    i = pl.program_id(0)
    p = c_ref[...].astype(F32) * xi_ref[...].astype(F32)
    halo = ch_ref[...].astype(F32) * xh_ref[...].astype(F32)
    halo = halo * jnp.where(i > 0, 1.0, 0.0)
    nh = halo.shape[0]
    prev1 = halo[nh - 1:nh, :]
    prev2 = halo[nh - 2:nh - 1, :]
    row = lax.broadcasted_iota(jnp.int32, p.shape, 0)
    p1 = jnp.where(row == 0, prev1, pltpu.roll(p, 1, 0))
    p2 = jnp.where(row == 0, prev2, jnp.where(row == 1, prev1, pltpu.roll(p, 2, 0)))
    w = w_ref[...]
    y = w[2:3, :] * p + w[1:2, :] * p1 + w[0:1, :] * p2
    o_ref[...] = (b_ref[...].astype(F32) * y).astype(o_ref.dtype)


def _conv(z, conv_w):
    l = z.shape[0]
    ch = conv_w.shape[1]
    tm, tc = _tile(l, 512), _tile(ch, 512)
    halo = 16
    nc = ch // tc
    rb = tm // halo
    return pl.pallas_call(
        _conv_kernel,
        grid=(l // tm, nc),
        in_specs=[pl.BlockSpec((tm, tc), lambda i, j: (i, j)),
                  pl.BlockSpec((tm, tc), lambda i, j: (i, nc + j)),
                  pl.BlockSpec((tm, tc), lambda i, j: (i, 2 * nc + j)),
                  pl.BlockSpec((halo, tc), lambda i, j: (jnp.maximum(i * rb - 1, 0), nc + j)),
                  pl.BlockSpec((halo, tc), lambda i, j: (jnp.maximum(i * rb - 1, 0), 2 * nc + j)),
                  pl.BlockSpec((CONV_K, tc), lambda i, j: (0, j))],
        out_specs=pl.BlockSpec((tm, tc), lambda i, j: (i, j)),
        out_shape=jax.ShapeDtypeStruct((l, ch), BF16),
        compiler_params=_params("arbitrary", "arbitrary"),
        name="gated_conv",
    )(z, z, z, z, z, conv_w)


def _attn_kernel(lam_ref, g_ref, q_ref, k_ref, v_ref, o_ref,
                 m1_ref, l1_ref, a1_ref, m2_ref, l2_ref, a2_ref,
                 sa_ref, sb_ref, pa_ref, pb_ref, ala_ref, alb_ref, *, tq, tk, lam_init):
    qi = pl.program_id(1)
    d = ATTN_HEAD_DIM
    q = q_ref[...].astype(F32) * (d ** -0.5 * math.log2(math.e))
    q1 = q[:, :d].astype(BF16)
    q2 = q[:, d:].astype(BF16)

    m1_ref[...] = jnp.full(m1_ref.shape, MASK_VALUE, F32)
    m2_ref[...] = jnp.full(m2_ref.shape, MASK_VALUE, F32)
    l1_ref[...] = jnp.zeros(l1_ref.shape, F32)
    l2_ref[...] = jnp.zeros(l2_ref.shape, F32)
    a1_ref[...] = jnp.zeros(a1_ref.shape, F32)
    a2_ref[...] = jnp.zeros(a2_ref.shape, F32)

    nt = (((1,), (1,)), ((), ()))
    nch = tk // LANES
    rc = min(tq, 64)
    last = (qi * tq) // tk
    diag_off = qi * tq - last * tk

    def softmax_map(bufs, mi, masked, m_ref, l_ref):
        s_ref, p_ref, al_ref = bufs
        for r0 in range(0, tq, rc):
            rows = slice(r0, r0 + rc)
            s = s_ref[mi, rows, :]
            if masked:
                keep = (lax.broadcasted_iota(jnp.int32, s.shape, 1)
                        <= lax.broadcasted_iota(jnp.int32, s.shape, 0) + (r0 + diag_off))
                s = jnp.where(keep, s, MASK_VALUE)
            cols = [s[:, i * LANES:(i + 1) * LANES] for i in range(nch)]
            mx = functools.reduce(jnp.maximum, cols)
            m_old = m_ref[rows, :]
            m_new = jnp.maximum(m_old, jnp.max(mx, axis=1, keepdims=True))
            alpha = jnp.exp2(m_old - m_new)
            ps = [jnp.exp2(c - m_new) for c in cols]
            l_ref[rows, :] = alpha * l_ref[rows, :] + functools.reduce(jnp.add, ps)
            p_ref[mi, rows, :] = jnp.concatenate([x.astype(BF16) for x in ps], axis=1)
            al_ref[mi, rows, :] = alpha
            m_ref[rows, :] = m_new

    def pv_map(bufs, mi, v, a_ref):
        _, p_ref, al_ref = bufs
        pv = jnp.dot(p_ref[mi], v, preferred_element_type=F32)
        alpha = al_ref[mi]
        a_ref[...] = jnp.concatenate([alpha, alpha], axis=1) * a_ref[...] + pv

    def scores(kb, bufs):
        s_ref = bufs[0]
        ks = pl.multiple_of(kb * tk, tk)
        k = k_ref[pl.ds(ks, tk), :]
        s_ref[0] = lax.dot_general(q1, k[:, :d], nt, preferred_element_type=F32)
        s_ref[1] = lax.dot_general(q2, k[:, d:], nt, preferred_element_type=F32)

    def consume(kb, bufs, masked):
        ks = pl.multiple_of(kb * tk, tk)
        v = v_ref[pl.ds(ks, tk), :]
        softmax_map(bufs, 0, masked, m1_ref, l1_ref)
        pv_map(bufs, 0, v, a1_ref)
        softmax_map(bufs, 1, masked, m2_ref, l2_ref)
        pv_map(bufs, 1, v, a2_ref)

    buf_a = (sa_ref, pa_ref, ala_ref)
    buf_b = (sb_ref, pb_ref, alb_ref)
    scores(0, buf_a)

    def body(j, carry):
        scores(2 * j + 1, buf_b)
        consume(2 * j, buf_a, False)
        scores(2 * j + 2, buf_a)
        consume(2 * j + 1, buf_b, False)
        return carry

    lax.fori_loop(0, last // 2, body, 0)

    @pl.when(last % 2 == 1)
    def _():
        scores(last, buf_b)
        consume(last - 1, buf_a, False)
        consume(last, buf_b, True)

    @pl.when(last % 2 == 0)
    def _():
        consume(last, buf_a, True)

    lv = lam_ref[...]
    lam = (jnp.exp(jnp.sum(lv[0:1, :] * lv[1:2, :], axis=-1, keepdims=True))
           - jnp.exp(jnp.sum(lv[2:3, :] * lv[3:4, :], axis=-1, keepdims=True)) + lam_init)
    l1 = jnp.sum(l1_ref[...], axis=1, keepdims=True)
    l2 = jnp.sum(l2_ref[...], axis=1, keepdims=True)
    o = a1_ref[...] / l1 - lam * (a2_ref[...] / l2)
    ms = jnp.mean(o * o, axis=-1, keepdims=True)
    o = o * lax.rsqrt(ms + SUBLN_EPS) * g_ref[...] * (1.0 - lam_init)
    o_ref[...] = o.astype(o_ref.dtype)


def _attention(z, lam_vecs, subln_g, lam_init, conv_ch, attn_width):
    l = z.shape[0]
    hw = 2 * ATTN_HEAD_DIM
    heads = attn_width // hw
    tq, tk = _tile(l, 512), _tile(l, 1024)
    qoff = 3 * conv_ch // hw
    koff = qoff + heads
    voff = koff + heads
    once = pl.Buffered(1)
    return pl.pallas_call(
        functools.partial(_attn_kernel, tq=tq, tk=tk, lam_init=lam_init),
        grid=(heads, l // tq),
        in_specs=[pl.BlockSpec((4, ATTN_HEAD_DIM), lambda h, i: (0, 0)),
                  pl.BlockSpec((1, hw), lambda h, i: (0, 0)),
                  pl.BlockSpec((tq, hw), lambda h, i: (i, qoff + h)),
                  pl.BlockSpec((l, hw), lambda h, i: (0, koff + h), pipeline_mode=once),
                  pl.BlockSpec((l, hw), lambda h, i: (0, voff + h), pipeline_mode=once)],
        out_specs=pl.BlockSpec((tq, hw), lambda h, i: (i, h)),
        out_shape=jax.ShapeDtypeStruct((l, attn_width), BF16),
        scratch_shapes=[pltpu.VMEM((tq, LANES), F32), pltpu.VMEM((tq, LANES), F32), pltpu.VMEM((tq, hw), F32),
                        pltpu.VMEM((tq, LANES), F32), pltpu.VMEM((tq, LANES), F32), pltpu.VMEM((tq, hw), F32),
                        pltpu.VMEM((2, tq, tk), F32), pltpu.VMEM((2, tq, tk), F32),
                        pltpu.VMEM((2, tq, tk), BF16), pltpu.VMEM((2, tq, tk), BF16),
                        pltpu.VMEM((2, tq, LANES), F32), pltpu.VMEM((2, tq, LANES), F32)],
        compiler_params=_params("arbitrary", "arbitrary"),
        name="diff_attn",
    )(lam_vecs, subln_g.reshape(1, hw), z, z, z)


def _s5_prep_kernel(are_ref, aim_ref, ldt_ref, bre_ref, bim_ref, bbre_ref, bbim_ref, pwre_ref, pwim_ref,
                    *, log2_sub):
    lam_re = jnp.minimum(are_ref[...], -1e-4)
    lam_im = aim_ref[...]
    dt = jnp.exp(ldt_ref[...])
    mag = jnp.exp(lam_re * dt)
    ab_re = mag * jnp.cos(lam_im * dt)
    ab_im = mag * jnp.sin(lam_im * dt)
    den = lam_re * lam_re + lam_im * lam_im
    nr, ni = ab_re - 1.0, ab_im
    f_re = (nr * lam_re + ni * lam_im) / den
    f_im = (ni * lam_re - nr * lam_im) / den
    b_re, b_im = bre_ref[...], bim_ref[...]
    bbre_ref[...] = f_re * b_re - f_im * b_im
    bbim_ref[...] = f_re * b_im + f_im * b_re
    pwre_ref[0] = ab_re
    pwim_ref[0] = ab_im
    sr, si = ab_re, ab_im
    for _ in range(log2_sub):
        sr, si = sr * sr - si * si, 2.0 * sr * si
    pr, pi = sr, si
    for k in range(1, SUBLANES + 1):
        pwre_ref[k] = pr
        pwim_ref[k] = pi
        pr, pi = pr * sr - pi * si, pr * si + pi * sr


def _s5_prepare(a_re, a_im, log_dt, b_re, b_im, c_re, c_im, sub_rows):
    g, p = a_re.shape
    h = S5_GROUP
    w = p * h
    rep = lambda t: jnp.repeat(t, h, axis=1)
    tg = _tile(g, 64)
    spec = pl.BlockSpec((tg, w), lambda i: (i, 0))
    npw = SUBLANES + 1
    spec3 = pl.BlockSpec((npw, tg, w), lambda i: (0, i, 0))
    log2_sub = sub_rows.bit_length() - 1
    assert 1 << log2_sub == sub_rows
    bb_re, bb_im, pw_re, pw_im = pl.pallas_call(
        functools.partial(_s5_prep_kernel, log2_sub=log2_sub),
        grid=(g // tg,),
        in_specs=[spec] * 5,
        out_specs=[spec, spec, spec3, spec3],
        out_shape=[jax.ShapeDtypeStruct((g, w), F32)] * 2 + [jax.ShapeDtypeStruct((npw, g, w), F32)] * 2,
        compiler_params=_params("arbitrary"),
        name="s5_prep",
    )(rep(a_re), rep(a_im), jnp.broadcast_to(log_dt[:, None], (g, w)),
      b_re.reshape(g, w), b_im.reshape(g, w))

    nb = g // GROUPS_PER_BLOCK
    eye = jnp.eye(GROUPS_PER_BLOCK, dtype=F32)

    def in_mat(bb):
        t = bb.reshape(nb, GROUPS_PER_BLOCK, p, h).transpose(0, 1, 3, 2)
        t = t[:, :, :, None, :] * eye[None, :, None, :, None]
        return t.reshape(nb, CH_BLOCK, ST_BLOCK)

    def out_mat(c):
        t = c.reshape(nb, GROUPS_PER_BLOCK, h, p).transpose(0, 1, 3, 2)
        t = t[:, :, :, None, :] * eye[None, :, None, :, None]
        return t.reshape(nb, ST_BLOCK, CH_BLOCK)

    b_mat = jnp.concatenate([in_mat(bb_re), in_mat(bb_im)], axis=2).astype(BF16)
    c_mat = jnp.concatenate([out_mat(c_re), -out_mat(c_im)], axis=1).astype(BF16)

    def powers(pw):
        return pw[:, :, ::h].reshape(npw, nb, ST_BLOCK).transpose(1, 0, 2)

    pr, pi = powers(pw_re), powers(pw_im)
    ones = jnp.ones((1, SUBLANES, 1), F32)
    t_idx = jnp.arange(SUBLANES)[None, :, None]
    tabs = [pr[:, 0:1, :] * ones, pi[:, 0:1, :] * ones]
    for dshift in (1, 2, 4):
        keep = (t_idx >= dshift).astype(F32)
        tabs += [pr[:, dshift:dshift + 1, :] * keep, pi[:, dshift:dshift + 1, :] * keep]
    tabs += [pr[:, 1:, :], pi[:, 1:, :]]
    return b_mat, c_mat, jnp.stack(tabs, axis=1)


def _s5_kernel(u_ref, pm_ref, pmt_ref, b_ref, c_ref, t_ref, d_ref, o_ref, st_ref, carry_ref, *, tt):
    ci = pl.program_id(1)
    nlb = ST_BLOCK // LANES
    sub_rows = tt // SUBLANES

    @pl.when(ci == 0)
    def _():
        carry_ref[...] = jnp.zeros(carry_ref.shape, F32)

    u = jnp.dot(pm_ref[...], u_ref[...], preferred_element_type=F32).astype(BF16)
    st_ref[...] = jnp.dot(u, b_ref[...], preferred_element_type=F32)

    def lanes(j):
        return (slice(j * LANES, (j + 1) * LANES),
                slice(ST_BLOCK + j * LANES, ST_BLOCK + (j + 1) * LANES))

    def scan(init, store):
        def step(r, x):
            row = pl.ds(pl.multiple_of(r * SUBLANES, SUBLANES), SUBLANES)
            out = []
            for j in range(nlb):
                lo, hi = lanes(j)
                ar, ai = t_ref[0, :, lo], t_ref[1, :, lo]
                xr, xi = x[2 * j], x[2 * j + 1]
                nr = ar * xr - ai * xi + st_ref[row, lo]
                ni = ar * xi + ai * xr + st_ref[row, hi]
                if store:
                    st_ref[row, lo] = nr
                    st_ref[row, hi] = ni
                out += [nr, ni]
            return tuple(out)

        return lax.fori_loop(0, sub_rows, step, init)

    zero = jnp.zeros((SUBLANES, LANES), F32)
    ends = scan((zero,) * (2 * nlb), False)

    first = lax.broadcasted_iota(jnp.int32, (SUBLANES, LANES), 0) == 0
    top = SUBLANES - 1
    init = []
    for j in range(nlb):
        lo, _ = lanes(j)
        yr, yi = ends[2 * j], ends[2 * j + 1]
        for n, dshift in enumerate((1, 2, 4)):
            ar, ai = t_ref[2 + 2 * n, :, lo], t_ref[3 + 2 * n, :, lo]
            rr = pltpu.roll(yr, dshift, 0)
            ri = pltpu.roll(yi, dshift, 0)
            yr, yi = yr + ar * rr - ai * ri, yi + ar * ri + ai * rr
        pr, pi = t_ref[8, :, lo], t_ref[9, :, lo]
        cr, cim = carry_ref[0, :, lo], carry_ref[1, :, lo]
        er = yr + pr * cr - pi * cim
        ei = yi + pr * cim + pi * cr
        init += [jnp.where(first, cr, pltpu.roll(er, 1, 0)), jnp.where(first, cim, pltpu.roll(ei, 1, 0))]
        carry_ref[0, :, lo] = jnp.broadcast_to(er[top:top + 1, :], er.shape)
        carry_ref[1, :, lo] = jnp.broadcast_to(ei[top:top + 1, :], ei.shape)

    scan(tuple(init), True)

    y = jnp.dot(st_ref[...].astype(BF16), c_ref[...], preferred_element_type=F32)
    y = y + d_ref[...] * u.astype(F32)
    g = jax.nn.gelu(y).astype(BF16)
    o_ref[...] = jnp.dot(pmt_ref[...], g, preferred_element_type=F32).astype(o_ref.dtype)


def _s5_tile(l):
    return _tile(l, 512)


def _s5(u, b_mat, c_mat, tabs, d_skip):
    l, d = u.shape
    nb = d // CH_BLOCK
    tt = _s5_tile(l)
    rho = jnp.arange(tt)
    pos = (rho % SUBLANES) * (tt // SUBLANES) + rho // SUBLANES
    pm = (pos[:, None] == jnp.arange(tt)[None, :]).astype(BF16)
    return pl.pallas_call(
        functools.partial(_s5_kernel, tt=tt),
        grid=(nb, l // tt),
        in_specs=[pl.BlockSpec((tt, CH_BLOCK), lambda b, i: (i, b)),
                  pl.BlockSpec((tt, tt), lambda b, i: (0, 0)),
                  pl.BlockSpec((tt, tt), lambda b, i: (0, 0)),
                  pl.BlockSpec((None, CH_BLOCK, 2 * ST_BLOCK), lambda b, i: (b, 0, 0)),
                  pl.BlockSpec((None, 2 * ST_BLOCK, CH_BLOCK), lambda b, i: (b, 0, 0)),
                  pl.BlockSpec((None, 10, SUBLANES, ST_BLOCK), lambda b, i: (b, 0, 0, 0)),
                  pl.BlockSpec((1, CH_BLOCK), lambda b, i: (0, b))],
        out_specs=pl.BlockSpec((tt, CH_BLOCK), lambda b, i: (i, b)),
        out_shape=jax.ShapeDtypeStruct((l, d), BF16),
        scratch_shapes=[pltpu.VMEM((tt, 2 * ST_BLOCK), F32),
                        pltpu.VMEM((2, SUBLANES, ST_BLOCK), F32)],
        compiler_params=_params("arbitrary", "arbitrary"),
        name="s5_scan",
    )(u, pm, pm.T, b_mat, c_mat, tabs, d_skip.reshape(1, d))


def kernel(x, c, w_ada, b_ada, ada_table, norm_mix, norm_ffn, norm_final, mix_w_in, conv_w, lambda_q1, lambda_k1, lambda_q2, lambda_k2, subln_g, mix_w_out, s5_a_re, s5_a_im, s5_log_dt, s5_b_re, s5_b_im, s5_c_re, s5_c_im, s5_d, glu_w1, glu_w2, ffn_w_gate, ffn_w_up, ffn_w_down):
    bsz, seq, d = x.shape
    assert bsz == 1
    depth = ada_table.shape[0]
    conv_ch = conv_w.shape[2]
    attn_width = mix_w_out.shape[1] - conv_ch
    hidden = ffn_w_gate.shape[2]
    hidden_pad = _round_up(hidden, 1024)
    tk_down = hidden_pad // 4
    pad = hidden_pad - hidden

    w_in = mix_w_in.astype(BF16)
    w_out = mix_w_out.astype(BF16)
    w_glu1 = glu_w1.astype(BF16)
    w_glu2 = glu_w2.astype(BF16)
    w_gate = jnp.pad(ffn_w_gate.astype(BF16), ((0, 0), (0, 0), (0, pad)))
    w_up = jnp.pad(ffn_w_up.astype(BF16), ((0, 0), (0, 0), (0, pad)))
    w_down = jnp.pad(ffn_w_down.astype(BF16), ((0, 0), (0, pad), (0, 0)))

    mods = _ada(c, w_ada, b_ada, ada_table)
    xs = x.reshape(seq, d)
    for l in range(depth):
        r = l * N_MOD
        h = _norm_mod(xs, norm_mix[l], mods, r + 0, r + 1)
        if l % 2 == 0:
            e = l // 2
            lam_init = 0.8 - 0.6 * math.exp(-0.3 * l)
            z = _matmul(h, w_in, e)
            y_conv = _conv(z, conv_w[e])
            lam_vecs = jnp.stack([lambda_q1[e], lambda_k1[e], lambda_q2[e], lambda_k2[e]])
            y_attn = _attention(z, lam_vecs, subln_g[e], lam_init, conv_ch, attn_width)
            xs = _mm2_res(y_conv, y_attn, w_out, e, xs, mods, r + 2)
        else:
            o = l // 2
            b_mat, c_mat, tabs = _s5_prepare(s5_a_re[o], s5_a_im[o], s5_log_dt[o],
                                             s5_b_re[o], s5_b_im[o], s5_c_re[o], s5_c_im[o],
                                             _s5_tile(seq) // SUBLANES)
            g = _s5(h, b_mat, c_mat, tabs, s5_d[o])
            xs = _glu_res(g, w_glu1, w_glu2, o, xs, mods, r + 2)
        h = _norm_mod(xs, norm_ffn[l], mods, r + 3, r + 4)
        act = _swiglu_up(h, w_gate, w_up, l)
        xs = _mm_res(act, w_down, l, xs, mods, r + 5, tk_down)
    return _final_norm(xs, norm_final).reshape(bsz, seq, d)
```

```python
import functools
import math

import jax
import jax.numpy as jnp
from jax import lax
from jax.experimental import pallas as pl
from jax.experimental.pallas import tpu as pltpu

F32 = jnp.float32
BF16 = jnp.bfloat16

N_MOD = 6
CONV_K = 3
ATTN_HEAD_DIM = 128
S5_GROUP = 16
S5_STATE = 64
NORM_EPS = 1e-6
SUBLN_EPS = 1e-5

GROUPS_PER_BLOCK = 16
CH_BLOCK = GROUPS_PER_BLOCK * S5_GROUP
ST_BLOCK = GROUPS_PER_BLOCK * S5_STATE
BLOCKS_PER_STEP = 2
SUBLANES = 8
LANES = 128
MASK_VALUE = -1e30
VMEM_LIMIT = 56 * 1024 * 1024


def _params(*sem):
    return pltpu.CompilerParams(dimension_semantics=sem, vmem_limit_bytes=VMEM_LIMIT)


def _tile(dim, pref):
    t = min(pref, dim)
    while dim % t:
        t //= 2
    return t


def _round_up(n, m):
    return -(-n // m) * m


def _ada_kernel(c_ref, w_ref, b_ref, t_ref, o_ref):
    c = c_ref[...]
    a = (c * jax.nn.sigmoid(c)).astype(BF16)
    r = jnp.dot(a, w_ref[...].astype(BF16), preferred_element_type=F32)
    o_ref[...] = r[0:1, :] + b_ref[...] + t_ref[...]


def _ada(c, w_ada, b_ada, ada_table):
    d = c.shape[1]
    depth = ada_table.shape[0]
    n = w_ada.shape[1]
    tn = _tile(n, 512)
    c8 = jnp.broadcast_to(c, (SUBLANES, d))
    out = pl.pallas_call(
        _ada_kernel,
        grid=(n // tn,),
        in_specs=[pl.BlockSpec((SUBLANES, d), lambda j: (0, 0)),
                  pl.BlockSpec((d, tn), lambda j: (0, j)),
                  pl.BlockSpec((1, tn), lambda j: (0, j)),
                  pl.BlockSpec((depth, tn), lambda j: (0, j))],
        out_specs=pl.BlockSpec((depth, tn), lambda j: (0, j)),
        out_shape=jax.ShapeDtypeStruct((depth, n), F32),
        compiler_params=_params("arbitrary"),
        name="ada_proj",
    )(c8, w_ada, b_ada.reshape(1, n), ada_table.reshape(depth, n))
    return out.reshape(depth * N_MOD, d)


def _norm_mod_kernel(x_ref, g_ref, m_ref, o_ref, *, shift_row, scale_row):
    x = x_ref[...]
    ms = jnp.mean(x * x, axis=-1, keepdims=True)
    y = x * lax.rsqrt(ms + NORM_EPS) * g_ref[...]
    shift = m_ref[shift_row:shift_row + 1, :]
    scale = m_ref[scale_row:scale_row + 1, :]
    o_ref[...] = (y * (1.0 + scale) + shift).astype(o_ref.dtype)


def _norm_mod(x, g, mods, shift_row, scale_row):
    l, d = x.shape
    tm = _tile(l, 512)
    return pl.pallas_call(
        functools.partial(_norm_mod_kernel, shift_row=shift_row, scale_row=scale_row),
        grid=(l // tm,),
        in_specs=[pl.BlockSpec((tm, d), lambda i: (i, 0)),
                  pl.BlockSpec((1, d), lambda i: (0, 0)),
                  pl.BlockSpec(mods.shape, lambda i: (0, 0))],
        out_specs=pl.BlockSpec((tm, d), lambda i: (i, 0)),
        out_shape=jax.ShapeDtypeStruct((l, d), BF16),
        compiler_params=_params("arbitrary"),
        name="norm_mod",
    )(x, g.reshape(1, d), mods)


def _norm_kernel(x_ref, g_ref, o_ref):
    x = x_ref[...]
    ms = jnp.mean(x * x, axis=-1, keepdims=True)
    o_ref[...] = x * lax.rsqrt(ms + NORM_EPS) * g_ref[...]


def _final_norm(x, g):
    l, d = x.shape
    tm = _tile(l, 512)
    return pl.pallas_call(
        _norm_kernel,
        grid=(l // tm,),
        in_specs=[pl.BlockSpec((tm, d), lambda i: (i, 0)),
                  pl.BlockSpec((1, d), lambda i: (0, 0))],
        out_specs=pl.BlockSpec((tm, d), lambda i: (i, 0)),
        out_shape=jax.ShapeDtypeStruct((l, d), F32),
        compiler_params=_params("arbitrary"),
        name="final_norm",
    )(x, g.reshape(1, d))


def _mm_kernel(a_ref, w_ref, o_ref):
    o_ref[...] = jnp.dot(a_ref[...], w_ref[...], preferred_element_type=F32).astype(o_ref.dtype)


def _matmul(a, w, layer):
    m, k = a.shape
    n = w.shape[2]
    tm, tn = _tile(m, 1024), _tile(n, 1024)
    return pl.pallas_call(
        _mm_kernel,
        grid=(m // tm, n // tn),
        in_specs=[pl.BlockSpec((tm, k), lambda i, j: (i, 0)),
                  pl.BlockSpec((None, k, tn), lambda i, j: (layer, 0, j))],
        out_specs=pl.BlockSpec((tm, tn), lambda i, j: (i, j)),
        out_shape=jax.ShapeDtypeStruct((m, n), BF16),
        compiler_params=_params("arbitrary", "arbitrary"),
        name="in_proj",
    )(a, w)


def _swiglu_up_kernel(a_ref, wg_ref, wu_ref, o_ref, *, hidden):
    a = a_ref[...]
    g = jnp.dot(a, wg_ref[...], preferred_element_type=F32)
    u = jnp.dot(a, wu_ref[...], preferred_element_type=F32)
    act = g * jax.nn.sigmoid(g) * u
    tn = act.shape[1]
    col = pl.program_id(1) * tn + lax.broadcasted_iota(jnp.int32, (1, tn), 1)
    o_ref[...] = jnp.where(col < hidden, act, 0.0).astype(o_ref.dtype)


def _swiglu_up(a, wg, wu, layer, n_pad):
    m, k = a.shape
    hidden = wg.shape[2]
    tm, tn = _tile(m, 1024), _tile(n_pad, 512)
    return pl.pallas_call(
        functools.partial(_swiglu_up_kernel, hidden=hidden),
        grid=(m // tm, n_pad // tn),
        in_specs=[pl.BlockSpec((tm, k), lambda i, j: (i, 0)),
                  pl.BlockSpec((None, k, tn), lambda i, j: (layer, 0, j)),
                  pl.BlockSpec((None, k, tn), lambda i, j: (layer, 0, j))],
        out_specs=pl.BlockSpec((tm, tn), lambda i, j: (i, j)),
        out_shape=jax.ShapeDtypeStruct((m, n_pad), BF16),
        compiler_params=_params("arbitrary", "arbitrary"),
        name="ffn_up",
    )(a, wg, wu)


def _glu_res_kernel(a_ref, w1_ref, w2_ref, x_ref, m_ref, o_ref, *, gate_row):
    a = a_ref[...]
    y1 = jnp.dot(a, w1_ref[...], preferred_element_type=F32)
    y2 = jnp.dot(a, w2_ref[...], preferred_element_type=F32)
    gate = m_ref[gate_row:gate_row + 1, :]
    o_ref[...] = x_ref[...] + gate * (y1 * jax.nn.sigmoid(y2))


def _glu_res(a, w1, w2, layer, x, mods, gate_row):
    m, k = a.shape
    n = w1.shape[2]
    tm, tn = _tile(m, 1024), _tile(n, 512)
    return pl.pallas_call(
        functools.partial(_glu_res_kernel, gate_row=gate_row),
        grid=(m // tm, n // tn),
        in_specs=[pl.BlockSpec((tm, k), lambda i, j: (i, 0)),
                  pl.BlockSpec((None, k, tn), lambda i, j: (layer, 0, j)),
                  pl.BlockSpec((None, k, tn), lambda i, j: (layer, 0, j)),
                  pl.BlockSpec((tm, tn), lambda i, j: (i, j)),
                  pl.BlockSpec((mods.shape[0], tn), lambda i, j: (0, j))],
        out_specs=pl.BlockSpec((tm, tn), lambda i, j: (i, j)),
        out_shape=jax.ShapeDtypeStruct((m, n), F32),
        compiler_params=_params("arbitrary", "arbitrary"),
        name="glu_res",
    )(a, w1, w2, x, mods)


def _mm_res_kernel(a_ref, w_ref, x_ref, m_ref, o_ref, acc_ref, *, gate_row, nk, k_valid):
    kk = pl.program_id(2)
    w = w_ref[...]
    tk = w.shape[0]
    row = kk * tk + lax.broadcasted_iota(jnp.int32, (tk, 1), 0)
    w = jnp.where(row < k_valid, w, jnp.zeros_like(w))
    part = jnp.dot(a_ref[...], w, preferred_element_type=F32)

    @pl.when(kk == 0)
    def _():
        acc_ref[...] = part

    @pl.when(kk > 0)
    def _():
        acc_ref[...] += part

    @pl.when(kk == nk - 1)
    def _():
        gate = m_ref[gate_row:gate_row + 1, :]
        o_ref[...] = x_ref[...] + gate * acc_ref[...]


def _mm_res(a, w, layer, x, mods, gate_row, tk):
    m, k = a.shape
    n = w.shape[2]
    tm, tn = _tile(m, 1024), _tile(n, 1024)
    nk = k // tk
    return pl.pallas_call(
        functools.partial(_mm_res_kernel, gate_row=gate_row, nk=nk, k_valid=w.shape[1]),
        grid=(m // tm, n // tn, nk),
        in_specs=[pl.BlockSpec((tm, tk), lambda i, j, kk: (i, kk)),
                  pl.BlockSpec((None, tk, tn), lambda i, j, kk: (layer, kk, j)),
                  pl.BlockSpec((tm, tn), lambda i, j, kk: (i, j)),
                  pl.BlockSpec((mods.shape[0], tn), lambda i, j, kk: (0, j))],
        out_specs=pl.BlockSpec((tm, tn), lambda i, j, kk: (i, j)),
        out_shape=jax.ShapeDtypeStruct((m, n), F32),
        scratch_shapes=[pltpu.VMEM((tm, tn), F32)],
        compiler_params=_params("arbitrary", "arbitrary", "arbitrary"),
        name="ffn_down",
    )(a, w, x, mods)


def _mm2_res_kernel(a1_ref, a2_ref, w1_ref, w2_ref, x_ref, m_ref, o_ref, *, gate_row):
    y = jnp.dot(a1_ref[...], w1_ref[...], preferred_element_type=F32)
    y = y + jnp.dot(a2_ref[...], w2_ref[...], preferred_element_type=F32)
    gate = m_ref[gate_row:gate_row + 1, :]
    o_ref[...] = x_ref[...] + gate * y


def _mm2_res(a1, a2, w, layer, x, mods, gate_row):
    m, k1 = a1.shape
    k2 = a2.shape[1]
    assert k1 == k2 and w.shape[1] == k1 + k2
    n = w.shape[2]
    tm, tn = _tile(m, 1024), _tile(n, 512)
    return pl.pallas_call(
        functools.partial(_mm2_res_kernel, gate_row=gate_row),
        grid=(m // tm, n // tn),
        in_specs=[pl.BlockSpec((tm, k1), lambda i, j: (i, 0)),
                  pl.BlockSpec((tm, k2), lambda i, j: (i, 0)),
                  pl.BlockSpec((None, k1, tn), lambda i, j: (layer, 0, j)),
                  pl.BlockSpec((None, k2, tn), lambda i, j: (layer, 1, j)),
                  pl.BlockSpec((tm, tn), lambda i, j: (i, j)),
                  pl.BlockSpec((mods.shape[0], tn), lambda i, j: (0, j))],
        out_specs=pl.BlockSpec((tm, tn), lambda i, j: (i, j)),
        out_shape=jax.ShapeDtypeStruct((m, n), F32),
        compiler_params=_params("arbitrary", "arbitrary"),
        name="out_proj",
    )(a1, a2, w, w, x, mods)


def _conv_kernel(b_ref, c_ref, xi_ref, ch_ref, xh_ref, w_ref, o_ref):
    i = pl.program_id(0)
    p = c_ref[...].astype(F32) * xi_ref[...].astype(F32)
    halo = ch_ref[...].astype(F32) * xh_ref[...].astype(F32)
    halo = halo * jnp.where(i > 0, 1.0, 0.0)
    nh = halo.shape[0]
    prev1 = halo[nh - 1:nh, :]
    prev2 = halo[nh - 2:nh - 1, :]
    row = lax.broadcasted_iota(jnp.int32, p.shape, 0)
    p1 = jnp.where(row == 0, prev1, pltpu.roll(p, 1, 0))
    p2 = jnp.where(row == 0, prev2, jnp.where(row == 1, prev1, pltpu.roll(p, 2, 0)))
    w = w_ref[...]
    y = w[2:3, :] * p + w[1:2, :] * p1 + w[0:1, :] * p2
    o_ref[...] = (b_ref[...].astype(F32) * y).astype(o_ref.dtype)


def _conv(z, conv_w):
    l = z.shape[0]
    ch = conv_w.shape[1]
    tm, tc = _tile(l, 512), _tile(ch, 512)
    halo = 16
    nc = ch // tc
    rb = tm // halo
    return pl.pallas_call(
        _conv_kernel,
        grid=(l // tm, nc),
        in_specs=[pl.BlockSpec((tm, tc), lambda i, j: (i, j)),
                  pl.BlockSpec((tm, tc), lambda i, j: (i, nc + j)),
                  pl.BlockSpec((tm, tc), lambda i, j: (i, 2 * nc + j)),
                  pl.BlockSpec((halo, tc), lambda i, j: (jnp.maximum(i * rb - 1, 0), nc + j)),
                  pl.BlockSpec((halo, tc), lambda i, j: (jnp.maximum(i * rb - 1, 0), 2 * nc + j)),
                  pl.BlockSpec((CONV_K, tc), lambda i, j: (0, j))],
        out_specs=pl.BlockSpec((tm, tc), lambda i, j: (i, j)),
        out_shape=jax.ShapeDtypeStruct((l, ch), BF16),
        compiler_params=_params("arbitrary", "arbitrary"),
        name="gated_conv",
    )(z, z, z, z, z, conv_w)


def _attn_kernel(lam_ref, g_ref, q_ref, k_ref, v_ref, o_ref,
                 m1_ref, l1_ref, a1_ref, m2_ref, l2_ref, a2_ref,
                 sa_ref, sb_ref, pa_ref, pb_ref, ala_ref, alb_ref, *, tq, tk, lam_init):
    qi = pl.program_id(1)
    d = ATTN_HEAD_DIM
    q = q_ref[...].astype(F32) * (d ** -0.5 * math.log2(math.e))
    q1 = q[:, :d].astype(BF16)
    q2 = q[:, d:].astype(BF16)

    m1_ref[...] = jnp.full(m1_ref.shape, MASK_VALUE, F32)
    m2_ref[...] = jnp.full(m2_ref.shape, MASK_VALUE, F32)
    l1_ref[...] = jnp.zeros(l1_ref.shape, F32)
    l2_ref[...] = jnp.zeros(l2_ref.shape, F32)
    a1_ref[...] = jnp.zeros(a1_ref.shape, F32)
    a2_ref[...] = jnp.zeros(a2_ref.shape, F32)

    nt = (((1,), (1,)), ((), ()))
    nch = tk // LANES
    rc = min(tq, 64)
    last = (qi * tq) // tk
    diag_off = qi * tq - last * tk

    def softmax_map(bufs, mi, masked, m_ref, l_ref):
        s_ref, p_ref, al_ref = bufs
        for r0 in range(0, tq, rc):
            rows = slice(r0, r0 + rc)
            s = s_ref[mi, rows, :]
            if masked:
                keep = (lax.broadcasted_iota(jnp.int32, s.shape, 1)
                        <= lax.broadcasted_iota(jnp.int32, s.shape, 0) + (r0 + diag_off))
                s = jnp.where(keep, s, MASK_VALUE)
            cols = [s[:, i * LANES:(i + 1) * LANES] for i in range(nch)]
            mx = functools.reduce(jnp.maximum, cols)
            m_old = m_ref[rows, :]
            m_new = jnp.maximum(m_old, jnp.max(mx, axis=1, keepdims=True))
            alpha = jnp.exp2(m_old - m_new)
            ps = [jnp.exp2(c - m_new) for c in cols]
            l_ref[rows, :] = alpha * l_ref[rows, :] + functools.reduce(jnp.add, ps)
            p_ref[mi, rows, :] = jnp.concatenate([x.astype(BF16) for x in ps], axis=1)
            al_ref[mi, rows, :] = alpha
            m_ref[rows, :] = m_new

    def pv_map(bufs, mi, v, a_ref):
        _, p_ref, al_ref = bufs
        pv = jnp.dot(p_ref[mi], v, preferred_element_type=F32)
        alpha = al_ref[mi]
        a_ref[...] = jnp.concatenate([alpha, alpha], axis=1) * a_ref[...] + pv

    def scores(kb, bufs):
        s_ref = bufs[0]
        ks = pl.multiple_of(kb * tk, tk)
        k = k_ref[pl.ds(ks, tk), :]
        s_ref[0] = lax.dot_general(q1, k[:, :d], nt, preferred_element_type=F32)
        s_ref[1] = lax.dot_general(q2, k[:, d:], nt, preferred_element_type=F32)

    def consume(kb, bufs, masked):
        ks = pl.multiple_of(kb * tk, tk)
        v = v_ref[pl.ds(ks, tk), :]
        softmax_map(bufs, 0, masked, m1_ref, l1_ref)
        pv_map(bufs, 0, v, a1_ref)
        softmax_map(bufs, 1, masked, m2_ref, l2_ref)
        pv_map(bufs, 1, v, a2_ref)

    buf_a = (sa_ref, pa_ref, ala_ref)
    buf_b = (sb_ref, pb_ref, alb_ref)
    scores(0, buf_a)

    def body(j, carry):
        scores(2 * j + 1, buf_b)
        consume(2 * j, buf_a, False)
        scores(2 * j + 2, buf_a)
        consume(2 * j + 1, buf_b, False)
        return carry

    lax.fori_loop(0, last // 2, body, 0)

    @pl.when(last % 2 == 1)
    def _():
        scores(last, buf_b)
        consume(last - 1, buf_a, False)
        consume(last, buf_b, True)

    @pl.when(last % 2 == 0)
    def _():
        consume(last, buf_a, True)

    lv = lam_ref[...]
    lam = (jnp.exp(jnp.sum(lv[0:1, :] * lv[1:2, :], axis=-1, keepdims=True))
           - jnp.exp(jnp.sum(lv[2:3, :] * lv[3:4, :], axis=-1, keepdims=True)) + lam_init)
    l1 = jnp.sum(l1_ref[...], axis=1, keepdims=True)
    l2 = jnp.sum(l2_ref[...], axis=1, keepdims=True)
    o = a1_ref[...] / l1 - lam * (a2_ref[...] / l2)
    ms = jnp.mean(o * o, axis=-1, keepdims=True)
    o = o * lax.rsqrt(ms + SUBLN_EPS) * g_ref[...] * (1.0 - lam_init)
    o_ref[...] = o.astype(o_ref.dtype)


def _attention(z, lam_vecs, subln_g, lam_init, conv_ch, attn_width):
    l = z.shape[0]
    hw = 2 * ATTN_HEAD_DIM
    heads = attn_width // hw
    tq, tk = _tile(l, 512), _tile(l, 1024)
    qoff = 3 * conv_ch // hw
    koff = qoff + heads
    voff = koff + heads
    once = pl.Buffered(1)
    return pl.pallas_call(
        functools.partial(_attn_kernel, tq=tq, tk=tk, lam_init=lam_init),
        grid=(heads, l // tq),
        in_specs=[pl.BlockSpec((4, ATTN_HEAD_DIM), lambda h, i: (0, 0)),
                  pl.BlockSpec((1, hw), lambda h, i: (0, 0)),
                  pl.BlockSpec((tq, hw), lambda h, i: (i, qoff + h)),
                  pl.BlockSpec((l, hw), lambda h, i: (0, koff + h), pipeline_mode=once),
                  pl.BlockSpec((l, hw), lambda h, i: (0, voff + h), pipeline_mode=once)],
        out_specs=pl.BlockSpec((tq, hw), lambda h, i: (i, h)),
        out_shape=jax.ShapeDtypeStruct((l, attn_width), BF16),
        scratch_shapes=[pltpu.VMEM((tq, LANES), F32), pltpu.VMEM((tq, LANES), F32), pltpu.VMEM((tq, hw), F32),
                        pltpu.VMEM((tq, LANES), F32), pltpu.VMEM((tq, LANES), F32), pltpu.VMEM((tq, hw), F32),
                        pltpu.VMEM((2, tq, tk), F32), pltpu.VMEM((2, tq, tk), F32),
                        pltpu.VMEM((2, tq, tk), BF16), pltpu.VMEM((2, tq, tk), BF16),
                        pltpu.VMEM((2, tq, LANES), F32), pltpu.VMEM((2, tq, LANES), F32)],
        compiler_params=_params("arbitrary", "arbitrary"),
        name="diff_attn",
    )(lam_vecs, subln_g.reshape(1, hw), z, z, z)


def _s5_prep_kernel(are_ref, aim_ref, ldt_ref, bre_ref, bim_ref, bbre_ref, bbim_ref, pwre_ref, pwim_ref,
                    *, log2_sub):
    lam_re = jnp.minimum(are_ref[...], -1e-4)
    lam_im = aim_ref[...]
    dt = jnp.exp(ldt_ref[...])
    mag = jnp.exp(lam_re * dt)
    ab_re = mag * jnp.cos(lam_im * dt)
    ab_im = mag * jnp.sin(lam_im * dt)
    den = lam_re * lam_re + lam_im * lam_im
    nr, ni = ab_re - 1.0, ab_im
    f_re = (nr * lam_re + ni * lam_im) / den
    f_im = (ni * lam_re - nr * lam_im) / den
    b_re, b_im = bre_ref[...], bim_ref[...]
    bbre_ref[...] = f_re * b_re - f_im * b_im
    bbim_ref[...] = f_re * b_im + f_im * b_re
    pwre_ref[0] = ab_re
    pwim_ref[0] = ab_im
    sr, si = ab_re, ab_im
    for _ in range(log2_sub):
        sr, si = sr * sr - si * si, 2.0 * sr * si
    pr, pi = sr, si
    for k in range(1, SUBLANES + 1):
        pwre_ref[k] = pr
        pwim_ref[k] = pi
        pr, pi = pr * sr - pi * si, pr * si + pi * sr


def _s5_prepare(a_re, a_im, log_dt, b_re, b_im, c_re, c_im, sub_rows):
    g, p = a_re.shape
    h = S5_GROUP
    w = p * h
    rep = lambda t: jnp.repeat(t, h, axis=1)
    tg = _tile(g, 64)
    spec = pl.BlockSpec((tg, w), lambda i: (i, 0))
    npw = SUBLANES + 1
    spec3 = pl.BlockSpec((npw, tg, w), lambda i: (0, i, 0))
    log2_sub = sub_rows.bit_length() - 1
    assert 1 << log2_sub == sub_rows
    bb_re, bb_im, pw_re, pw_im = pl.pallas_call(
        functools.partial(_s5_prep_kernel, log2_sub=log2_sub),
        grid=(g // tg,),
        in_specs=[spec] * 5,
        out_specs=[spec, spec, spec3, spec3],
        out_shape=[jax.ShapeDtypeStruct((g, w), F32)] * 2 + [jax.ShapeDtypeStruct((npw, g, w), F32)] * 2,
        compiler_params=_params("arbitrary"),
        name="s5_prep",
    )(rep(a_re), rep(a_im), jnp.broadcast_to(log_dt[:, None], (g, w)),
      b_re.reshape(g, w), b_im.reshape(g, w))

    nb = g // GROUPS_PER_BLOCK
    eye = jnp.eye(GROUPS_PER_BLOCK, dtype=F32)

    def in_mat(bb):
        t = bb.reshape(nb, GROUPS_PER_BLOCK, p, h).transpose(0, 1, 3, 2)
        t = t[:, :, :, None, :] * eye[None, :, None, :, None]
        return t.reshape(nb, CH_BLOCK, ST_BLOCK)

    def out_mat(c):
        t = c.reshape(nb, GROUPS_PER_BLOCK, h, p).transpose(0, 1, 3, 2)
        t = t[:, :, :, None, :] * eye[None, :, None, :, None]
        return t.reshape(nb, ST_BLOCK, CH_BLOCK)

    b_mat = jnp.concatenate([in_mat(bb_re), in_mat(bb_im)], axis=2).astype(BF16)
    c_mat = jnp.concatenate([out_mat(c_re), -out_mat(c_im)], axis=1).astype(BF16)

    def powers(pw):
        return pw[:, :, ::h].reshape(npw, nb, ST_BLOCK).transpose(1, 0, 2)

    pr, pi = powers(pw_re), powers(pw_im)
    ones = jnp.ones((1, SUBLANES, 1), F32)
    t_idx = jnp.arange(SUBLANES)[None, :, None]
    tabs = [pr[:, 0:1, :] * ones, pi[:, 0:1, :] * ones]
    for dshift in (1, 2, 4):
        keep = (t_idx >= dshift).astype(F32)
        tabs += [pr[:, dshift:dshift + 1, :] * keep, pi[:, dshift:dshift + 1, :] * keep]
    tabs += [pr[:, 1:, :], pi[:, 1:, :]]
    return b_mat, c_mat, jnp.stack(tabs, axis=1)


def _s5_kernel(u_ref, pm_ref, pmt_ref, b_ref, c_ref, t_ref, d_ref, o_ref, st_ref, carry_ref, *, tt):
    ci = pl.program_id(1)
    nlb = ST_BLOCK // LANES
    sub_rows = tt // SUBLANES

    @pl.when(ci == 0)
    def _():
        carry_ref[...] = jnp.zeros(carry_ref.shape, F32)

    def lanes(j):
        return (slice(j * LANES, (j + 1) * LANES),
                slice(ST_BLOCK + j * LANES, ST_BLOCK + (j + 1) * LANES))

    def project_in(h):
        ch = slice(h * CH_BLOCK, (h + 1) * CH_BLOCK)
        u = jnp.dot(pm_ref[...], u_ref[:, ch], preferred_element_type=F32).astype(BF16)
        st_ref[h] = jnp.dot(u, b_ref[h], preferred_element_type=F32)
        return u

    def scan(h, init, store):
        def step(r, x):
            row = slice(r * SUBLANES, (r + 1) * SUBLANES)
            out = []
            for j in range(nlb):
                lo, hi = lanes(j)
                ar, ai = t_ref[h, 0, :, lo], t_ref[h, 1, :, lo]
                xr, xi = x[2 * j], x[2 * j + 1]
                nr = ar * xr - ai * xi + st_ref[h, row, lo]
                ni = ar * xi + ai * xr + st_ref[h, row, hi]
                if store:
                    st_ref[h, row, lo] = nr
                    st_ref[h, row, hi] = ni
                out += [nr, ni]
            return tuple(out)

        x = init
        for r in range(sub_rows):
            x = step(r, x)
        return x

    def start_states(h, ends):
        first = lax.broadcasted_iota(jnp.int32, (SUBLANES, LANES), 0) == 0
        top = SUBLANES - 1
        init = []
        for j in range(nlb):
            lo, _ = lanes(j)
            yr, yi = ends[2 * j], ends[2 * j + 1]
            for n, dshift in enumerate((1, 2, 4)):
                ar, ai = t_ref[h, 2 + 2 * n, :, lo], t_ref[h, 3 + 2 * n, :, lo]
                rr = pltpu.roll(yr, dshift, 0)
                ri = pltpu.roll(yi, dshift, 0)
                yr, yi = yr + ar * rr - ai * ri, yi + ar * ri + ai * rr
            pr, pi = t_ref[h, 8, :, lo], t_ref[h, 9, :, lo]
            cr, cim = carry_ref[h, 0, :, lo], carry_ref[h, 1, :, lo]
            er = yr + pr * cr - pi * cim
            ei = yi + pr * cim + pi * cr
            init += [jnp.where(first, cr, pltpu.roll(er, 1, 0)),
                     jnp.where(first, cim, pltpu.roll(ei, 1, 0))]
            carry_ref[h, 0, :, lo] = jnp.broadcast_to(er[top:top + 1, :], er.shape)
            carry_ref[h, 1, :, lo] = jnp.broadcast_to(ei[top:top + 1, :], ei.shape)
        return tuple(init)

    def project_out(h, u):
        ch = slice(h * CH_BLOCK, (h + 1) * CH_BLOCK)
        y = jnp.dot(st_ref[h].astype(BF16), c_ref[h], preferred_element_type=F32)
        y = y + d_ref[:, ch] * u.astype(F32)
        g = jax.nn.gelu(y).astype(BF16)
        o_ref[:, ch] = jnp.dot(pmt_ref[...], g, preferred_element_type=F32).astype(o_ref.dtype)

    zero = (jnp.zeros((SUBLANES, LANES), F32),) * (2 * nlb)
    us = [project_in(h) for h in range(BLOCKS_PER_STEP)]
    for h in range(BLOCKS_PER_STEP):
        ends = scan(h, zero, False)
        scan(h, start_states(h, ends), True)
        project_out(h, us[h])


def _s5_tile(l):
    return _tile(l, 512)


def _s5(u, b_mat, c_mat, tabs, d_skip):
    l, d = u.shape
    nb = d // CH_BLOCK
    bps = BLOCKS_PER_STEP
    assert nb % bps == 0
    tt = _s5_tile(l)
    rho = jnp.arange(tt)
    pos = (rho % SUBLANES) * (tt // SUBLANES) + rho // SUBLANES
    pm = (pos[:, None] == jnp.arange(tt)[None, :]).astype(BF16)
    return pl.pallas_call(
        functools.partial(_s5_kernel, tt=tt),
        grid=(nb // bps, l // tt),
        in_specs=[pl.BlockSpec((tt, bps * CH_BLOCK), lambda b, i: (i, b)),
                  pl.BlockSpec((tt, tt), lambda b, i: (0, 0)),
                  pl.BlockSpec((tt, tt), lambda b, i: (0, 0)),
                  pl.BlockSpec((bps, CH_BLOCK, 2 * ST_BLOCK), lambda b, i: (b, 0, 0)),
                  pl.BlockSpec((bps, 2 * ST_BLOCK, CH_BLOCK), lambda b, i: (b, 0, 0)),
                  pl.BlockSpec((bps, 10, SUBLANES, ST_BLOCK), lambda b, i: (b, 0, 0, 0)),
                  pl.BlockSpec((1, bps * CH_BLOCK), lambda b, i: (0, b))],
        out_specs=pl.BlockSpec((tt, bps * CH_BLOCK), lambda b, i: (i, b)),
        out_shape=jax.ShapeDtypeStruct((l, d), BF16),
        scratch_shapes=[pltpu.VMEM((bps, tt, 2 * ST_BLOCK), F32),
                        pltpu.VMEM((bps, 2, SUBLANES, ST_BLOCK), F32)],
        compiler_params=_params("arbitrary", "arbitrary"),
        name="s5_scan",
    )(u, pm, pm.T, b_mat, c_mat, tabs, d_skip.reshape(1, d))


def kernel(x, c, w_ada, b_ada, ada_table, norm_mix, norm_ffn, norm_final, mix_w_in, conv_w, lambda_q1, lambda_k1, lambda_q2, lambda_k2, subln_g, mix_w_out, s5_a_re, s5_a_im, s5_log_dt, s5_b_re, s5_b_im, s5_c_re, s5_c_im, s5_d, glu_w1, glu_w2, ffn_w_gate, ffn_w_up, ffn_w_down):
    bsz, seq, d = x.shape
    assert bsz == 1
    depth = ada_table.shape[0]
    conv_ch = conv_w.shape[2]
    attn_width = mix_w_out.shape[1] - conv_ch
    hidden = ffn_w_gate.shape[2]
    hidden_pad = _round_up(hidden, 512)
    tk_down = hidden_pad // 4
    assert tk_down % LANES == 0 and hidden_pad - hidden < min(512, tk_down)

    w_in = mix_w_in.astype(BF16)
    w_out = mix_w_out.astype(BF16)
    w_glu1 = glu_w1.astype(BF16)
    w_glu2 = glu_w2.astype(BF16)
    w_gate = ffn_w_gate.astype(BF16)
    w_up = ffn_w_up.astype(BF16)
    w_down = ffn_w_down.astype(BF16)

    mods = _ada(c, w_ada, b_ada, ada_table)
    xs = x.reshape(seq, d)
    for l in range(depth):
        r = l * N_MOD
        h = _norm_mod(xs, norm_mix[l], mods, r + 0, r + 1)
        if l % 2 == 0:
            e = l // 2
            lam_init = 0.8 - 0.6 * math.exp(-0.3 * l)
            z = _matmul(h, w_in, e)
            y_conv = _conv(z, conv_w[e])
            lam_vecs = jnp.stack([lambda_q1[e], lambda_k1[e], lambda_q2[e], lambda_k2[e]])
            y_attn = _attention(z, lam_vecs, subln_g[e], lam_init, conv_ch, attn_width)
            xs = _mm2_res(y_conv, y_attn, w_out, e, xs, mods, r + 2)
        else:
            o = l // 2
            b_mat, c_mat, tabs = _s5_prepare(s5_a_re[o], s5_a_im[o], s5_log_dt[o],
                                             s5_b_re[o], s5_b_im[o], s5_c_re[o], s5_c_im[o],
                                             _s5_tile(seq) // SUBLANES)
            g = _s5(h, b_mat, c_mat, tabs, s5_d[o])
            xs = _glu_res(g, w_glu1, w_glu2, o, xs, mods, r + 2)
        h = _norm_mod(xs, norm_ffn[l], mods, r + 3, r + 4)
        act = _swiglu_up(h, w_gate, w_up, l, hidden_pad)
        xs = _mm_res(act, w_down, l, xs, mods, r + 5, tk_down)
    return _final_norm(xs, norm_final).reshape(bsz, seq, d)
```

```python
import functools
import math

import jax
import jax.numpy as jnp
from jax import lax
from jax.experimental import pallas as pl
from jax.experimental.pallas import tpu as pltpu

F32 = jnp.float32
BF16 = jnp.bfloat16

N_MOD = 6
CONV_K = 3
ATTN_HEAD_DIM = 128
S5_GROUP = 16
S5_STATE = 64
NORM_EPS = 1e-6
SUBLN_EPS = 1e-5

GROUPS_PER_BLOCK = 16
CH_BLOCK = GROUPS_PER_BLOCK * S5_GROUP
ST_BLOCK = GROUPS_PER_BLOCK * S5_STATE
BLOCKS_PER_STEP = 2
SUBLANES = 8
LANES = 128
MASK_VALUE = -1e30
MIN_NORMALISER = 2.0 ** -80
VMEM_LIMIT = 56 * 1024 * 1024


def _params(*sem):
    return pltpu.CompilerParams(dimension_semantics=sem, vmem_limit_bytes=VMEM_LIMIT)


def _tile(dim, pref):
    t = min(pref, dim)
    while dim % t:
        t //= 2
    return t


def _round_up(n, m):
    return -(-n // m) * m


def _ada_kernel(c_ref, w_ref, b_ref, t_ref, o_ref):
    c = c_ref[...]
    a = (c * jax.nn.sigmoid(c)).astype(BF16)
    r = jnp.dot(a, w_ref[...].astype(BF16), preferred_element_type=F32)
    o_ref[...] = r[0:1, :] + b_ref[...] + t_ref[...]


def _ada(c, w_ada, b_ada, ada_table):
    d = c.shape[1]
    depth = ada_table.shape[0]
    n = w_ada.shape[1]
    tn = _tile(n, 512)
    c8 = jnp.broadcast_to(c, (SUBLANES, d))
    out = pl.pallas_call(
        _ada_kernel,
        grid=(n // tn,),
        in_specs=[pl.BlockSpec((SUBLANES, d), lambda j: (0, 0)),
                  pl.BlockSpec((d, tn), lambda j: (0, j)),
                  pl.BlockSpec((1, tn), lambda j: (0, j)),
                  pl.BlockSpec((depth, tn), lambda j: (0, j))],
        out_specs=pl.BlockSpec((depth, tn), lambda j: (0, j)),
        out_shape=jax.ShapeDtypeStruct((depth, n), F32),
        compiler_params=_params("arbitrary"),
        name="ada_proj",
    )(c8, w_ada, b_ada.reshape(1, n), ada_table.reshape(depth, n))
    return out.reshape(depth * N_MOD, d)


def _norm_mod_kernel(x_ref, g_ref, m_ref, o_ref, *, shift_row, scale_row):
    x = x_ref[...]
    ms = jnp.mean(x * x, axis=-1, keepdims=True)
    y = x * lax.rsqrt(ms + NORM_EPS) * g_ref[...]
    shift = m_ref[shift_row:shift_row + 1, :]
    scale = m_ref[scale_row:scale_row + 1, :]
    o_ref[...] = (y * (1.0 + scale) + shift).astype(o_ref.dtype)


def _norm_mod(x, g, mods, shift_row, scale_row):
    l, d = x.shape
    tm = _tile(l, 512)
    return pl.pallas_call(
        functools.partial(_norm_mod_kernel, shift_row=shift_row, scale_row=scale_row),
        grid=(l // tm,),
        in_specs=[pl.BlockSpec((tm, d), lambda i: (i, 0)),
                  pl.BlockSpec((1, d), lambda i: (0, 0)),
                  pl.BlockSpec(mods.shape, lambda i: (0, 0))],
        out_specs=pl.BlockSpec((tm, d), lambda i: (i, 0)),
        out_shape=jax.ShapeDtypeStruct((l, d), BF16),
        compiler_params=_params("arbitrary"),
        name="norm_mod",
    )(x, g.reshape(1, d), mods)


def _norm_kernel(x_ref, g_ref, o_ref):
    x = x_ref[...]
    ms = jnp.mean(x * x, axis=-1, keepdims=True)
    o_ref[...] = x * lax.rsqrt(ms + NORM_EPS) * g_ref[...]


def _final_norm(x, g):
    l, d = x.shape
    tm = _tile(l, 512)
    return pl.pallas_call(
        _norm_kernel,
        grid=(l // tm,),
        in_specs=[pl.BlockSpec((tm, d), lambda i: (i, 0)),
                  pl.BlockSpec((1, d), lambda i: (0, 0))],
        out_specs=pl.BlockSpec((tm, d), lambda i: (i, 0)),
        out_shape=jax.ShapeDtypeStruct((l, d), F32),
        compiler_params=_params("arbitrary"),
        name="final_norm",
    )(x, g.reshape(1, d))


def _mm_kernel(a_ref, w_ref, o_ref):
    o_ref[...] = jnp.dot(a_ref[...], w_ref[...], preferred_element_type=F32).astype(o_ref.dtype)


def _matmul(a, w, layer):
    m, k = a.shape
    n = w.shape[2]
    tm, tn = _tile(m, 1024), _tile(n, 1024)
    return pl.pallas_call(
        _mm_kernel,
        grid=(m // tm, n // tn),
        in_specs=[pl.BlockSpec((tm, k), lambda i, j: (i, 0)),
                  pl.BlockSpec((None, k, tn), lambda i, j: (layer, 0, j))],
        out_specs=pl.BlockSpec((tm, tn), lambda i, j: (i, j)),
        out_shape=jax.ShapeDtypeStruct((m, n), BF16),
        compiler_params=_params("arbitrary", "arbitrary"),
        name="in_proj",
    )(a, w)


def _swiglu_up_kernel(a_ref, wg_ref, wu_ref, o_ref, *, hidden):
    a = a_ref[...]
    g = jnp.dot(a, wg_ref[...], preferred_element_type=F32)
    u = jnp.dot(a, wu_ref[...], preferred_element_type=F32)
    act = g * jax.nn.sigmoid(g) * u
    tn = act.shape[1]
    col = pl.program_id(1) * tn + lax.broadcasted_iota(jnp.int32, (1, tn), 1)
    o_ref[...] = jnp.where(col < hidden, act, 0.0).astype(o_ref.dtype)


def _swiglu_up(a, wg, wu, layer, n_pad):
    m, k = a.shape
    hidden = wg.shape[2]
    tm, tn = _tile(m, 1024), _tile(n_pad, 512)
    return pl.pallas_call(
        functools.partial(_swiglu_up_kernel, hidden=hidden),
        grid=(m // tm, n_pad // tn),
        in_specs=[pl.BlockSpec((tm, k), lambda i, j: (i, 0)),
                  pl.BlockSpec((None, k, tn), lambda i, j: (layer, 0, j)),
                  pl.BlockSpec((None, k, tn), lambda i, j: (layer, 0, j))],
        out_specs=pl.BlockSpec((tm, tn), lambda i, j: (i, j)),
        out_shape=jax.ShapeDtypeStruct((m, n_pad), BF16),
        compiler_params=_params("arbitrary", "arbitrary"),
        name="ffn_up",
    )(a, wg, wu)


def _glu_res_kernel(a_ref, w1_ref, w2_ref, x_ref, m_ref, o_ref, *, gate_row):
    a = a_ref[...]
    y1 = jnp.dot(a, w1_ref[...], preferred_element_type=F32)
    y2 = jnp.dot(a, w2_ref[...], preferred_element_type=F32)
    gate = m_ref[gate_row:gate_row + 1, :]
    o_ref[...] = x_ref[...] + gate * (y1 * jax.nn.sigmoid(y2))


def _glu_res(a, w1, w2, layer, x, mods, gate_row):
    m, k = a.shape
    n = w1.shape[2]
    tm, tn = _tile(m, 1024), _tile(n, 512)
    return pl.pallas_call(
        functools.partial(_glu_res_kernel, gate_row=gate_row),
        grid=(m // tm, n // tn),
        in_specs=[pl.BlockSpec((tm, k), lambda i, j: (i, 0)),
                  pl.BlockSpec((None, k, tn), lambda i, j: (layer, 0, j)),
                  pl.BlockSpec((None, k, tn), lambda i, j: (layer, 0, j)),
                  pl.BlockSpec((tm, tn), lambda i, j: (i, j)),
                  pl.BlockSpec((mods.shape[0], tn), lambda i, j: (0, j))],
        out_specs=pl.BlockSpec((tm, tn), lambda i, j: (i, j)),
        out_shape=jax.ShapeDtypeStruct((m, n), F32),
        compiler_params=_params("arbitrary", "arbitrary"),
        name="glu_res",
    )(a, w1, w2, x, mods)


def _mm_res_kernel(a_ref, w_ref, x_ref, m_ref, o_ref, acc_ref, *, gate_row, nk, k_valid):
    kk = pl.program_id(2)
    w = w_ref[...]
    tk = w.shape[0]
    row = kk * tk + lax.broadcasted_iota(jnp.int32, (tk, 1), 0)
    w = jnp.where(row < k_valid, w, jnp.zeros_like(w))
    part = jnp.dot(a_ref[...], w, preferred_element_type=F32)

    @pl.when(kk == 0)
    def _():
        acc_ref[...] = part

    @pl.when(kk > 0)
    def _():
        acc_ref[...] += part

    @pl.when(kk == nk - 1)
    def _():
        gate = m_ref[gate_row:gate_row + 1, :]
        o_ref[...] = x_ref[...] + gate * acc_ref[...]


def _mm_res(a, w, layer, x, mods, gate_row, tk):
    m, k = a.shape
    n = w.shape[2]
    tm, tn = _tile(m, 1024), _tile(n, 1024)
    nk = k // tk
    return pl.pallas_call(
        functools.partial(_mm_res_kernel, gate_row=gate_row, nk=nk, k_valid=w.shape[1]),
        grid=(m // tm, n // tn, nk),
        in_specs=[pl.BlockSpec((tm, tk), lambda i, j, kk: (i, kk)),
                  pl.BlockSpec((None, tk, tn), lambda i, j, kk: (layer, kk, j)),
                  pl.BlockSpec((tm, tn), lambda i, j, kk: (i, j)),
                  pl.BlockSpec((mods.shape[0], tn), lambda i, j, kk: (0, j))],
        out_specs=pl.BlockSpec((tm, tn), lambda i, j, kk: (i, j)),
        out_shape=jax.ShapeDtypeStruct((m, n), F32),
        scratch_shapes=[pltpu.VMEM((tm, tn), F32)],
        compiler_params=_params("arbitrary", "arbitrary", "arbitrary"),
        name="ffn_down",
    )(a, w, x, mods)


def _mm2_res_kernel(a1_ref, a2_ref, w1_ref, w2_ref, x_ref, m_ref, o_ref, *, gate_row):
    y = jnp.dot(a1_ref[...], w1_ref[...], preferred_element_type=F32)
    y = y + jnp.dot(a2_ref[...], w2_ref[...], preferred_element_type=F32)
    gate = m_ref[gate_row:gate_row + 1, :]
    o_ref[...] = x_ref[...] + gate * y


def _mm2_res(a1, a2, w, layer, x, mods, gate_row):
    m, k1 = a1.shape
    k2 = a2.shape[1]
    assert k1 == k2 and w.shape[1] == k1 + k2
    n = w.shape[2]
    tm, tn = _tile(m, 1024), _tile(n, 512)
    return pl.pallas_call(
        functools.partial(_mm2_res_kernel, gate_row=gate_row),
        grid=(m // tm, n // tn),
        in_specs=[pl.BlockSpec((tm, k1), lambda i, j: (i, 0)),
                  pl.BlockSpec((tm, k2), lambda i, j: (i, 0)),
                  pl.BlockSpec((None, k1, tn), lambda i, j: (layer, 0, j)),
                  pl.BlockSpec((None, k2, tn), lambda i, j: (layer, 1, j)),
                  pl.BlockSpec((tm, tn), lambda i, j: (i, j)),
                  pl.BlockSpec((mods.shape[0], tn), lambda i, j: (0, j))],
        out_specs=pl.BlockSpec((tm, tn), lambda i, j: (i, j)),
        out_shape=jax.ShapeDtypeStruct((m, n), F32),
        compiler_params=_params("arbitrary", "arbitrary"),
        name="out_proj",
    )(a1, a2, w, w, x, mods)


def _conv_kernel(b_ref, c_ref, xi_ref, ch_ref, xh_ref, w_ref, o_ref):
    i = pl.program_id(0)
    p = c_ref[...].astype(F32) * xi_ref[...].astype(F32)
    halo = ch_ref[...].astype(F32) * xh_ref[...].astype(F32)
    halo = halo * jnp.where(i > 0, 1.0, 0.0)
    nh = halo.shape[0]
    prev1 = halo[nh - 1:nh, :]
    prev2 = halo[nh - 2:nh - 1, :]
    row = lax.broadcasted_iota(jnp.int32, p.shape, 0)
    p1 = jnp.where(row == 0, prev1, pltpu.roll(p, 1, 0))
    p2 = jnp.where(row == 0, prev2, jnp.where(row == 1, prev1, pltpu.roll(p, 2, 0)))
    w = w_ref[...]
    y = w[2:3, :] * p + w[1:2, :] * p1 + w[0:1, :] * p2
    o_ref[...] = (b_ref[...].astype(F32) * y).astype(o_ref.dtype)


def _conv(z, conv_w):
    l = z.shape[0]
    ch = conv_w.shape[1]
    tm, tc = _tile(l, 512), _tile(ch, 512)
    halo = 16
    nc = ch // tc
    rb = tm // halo
    return pl.pallas_call(
        _conv_kernel,
        grid=(l // tm, nc),
        in_specs=[pl.BlockSpec((tm, tc), lambda i, j: (i, j)),
                  pl.BlockSpec((tm, tc), lambda i, j: (i, nc + j)),
                  pl.BlockSpec((tm, tc), lambda i, j: (i, 2 * nc + j)),
                  pl.BlockSpec((halo, tc), lambda i, j: (jnp.maximum(i * rb - 1, 0), nc + j)),
                  pl.BlockSpec((halo, tc), lambda i, j: (jnp.maximum(i * rb - 1, 0), 2 * nc + j)),
                  pl.BlockSpec((CONV_K, tc), lambda i, j: (0, j))],
        out_specs=pl.BlockSpec((tm, tc), lambda i, j: (i, j)),
        out_shape=jax.ShapeDtypeStruct((l, ch), BF16),
        compiler_params=_params("arbitrary", "arbitrary"),
        name="gated_conv",
    )(z, z, z, z, z, conv_w)


def _attn_kernel(lam_ref, g_ref, q_ref, k_ref, v_ref, o_ref,
                 m1_ref, l1_ref, a1_ref, m2_ref, l2_ref, a2_ref,
                 sa_ref, sb_ref, pa_ref, pb_ref, ala_ref, alb_ref, kn_ref, *, tq, tk, lam_init):
    qi = pl.program_id(1)
    d = ATTN_HEAD_DIM
    q = q_ref[...].astype(F32) * (d ** -0.5 * math.log2(math.e))
    q1 = q[:, :d].astype(BF16)
    q2 = q[:, d:].astype(BF16)

    def reset():
        m1_ref[...] = jnp.full(m1_ref.shape, MASK_VALUE, F32)
        m2_ref[...] = jnp.full(m2_ref.shape, MASK_VALUE, F32)
        l1_ref[...] = jnp.zeros(l1_ref.shape, F32)
        l2_ref[...] = jnp.zeros(l2_ref.shape, F32)
        a1_ref[...] = jnp.zeros(a1_ref.shape, F32)
        a2_ref[...] = jnp.zeros(a2_ref.shape, F32)

    nt = (((1,), (1,)), ((), ()))
    nch = tk // LANES
    rc = min(tq, 64)
    last = (qi * tq) // tk
    diag_off = qi * tq - last * tk

    def softmax_map(bufs, mi, masked, m_ref, l_ref):
        s_ref, p_ref, al_ref = bufs
        for r0 in range(0, tq, rc):
            rows = slice(r0, r0 + rc)
            s = s_ref[mi, rows, :]
            if masked:
                keep = (lax.broadcasted_iota(jnp.int32, s.shape, 1)
                        <= lax.broadcasted_iota(jnp.int32, s.shape, 0) + (r0 + diag_off))
                s = jnp.where(keep, s, MASK_VALUE)
            cols = [s[:, i * LANES:(i + 1) * LANES] for i in range(nch)]
            mx = functools.reduce(jnp.maximum, cols)
            m_old = m_ref[rows, :]
            m_new = jnp.maximum(m_old, jnp.max(mx, axis=1, keepdims=True))
            alpha = jnp.exp2(m_old - m_new)
            ps = [jnp.exp2(c - m_new) for c in cols]
            l_ref[rows, :] = alpha * l_ref[rows, :] + functools.reduce(jnp.add, ps)
            p_ref[mi, rows, :] = jnp.concatenate([x.astype(BF16) for x in ps], axis=1)
            al_ref[mi, rows, :] = alpha
            m_ref[rows, :] = m_new

    def pv_map(bufs, mi, v, a_ref):
        _, p_ref, al_ref = bufs
        pv = jnp.dot(p_ref[mi], v, preferred_element_type=F32)
        alpha = al_ref[mi]
        a_ref[...] = jnp.concatenate([alpha, alpha], axis=1) * a_ref[...] + pv

    def scores(kb, bufs):
        s_ref = bufs[0]
        ks = pl.multiple_of(kb * tk, tk)
        k = k_ref[pl.ds(ks, tk), :]
        s_ref[0] = lax.dot_general(q1, k[:, :d], nt, preferred_element_type=F32)
        s_ref[1] = lax.dot_general(q2, k[:, d:], nt, preferred_element_type=F32)

    def consume(kb, bufs, masked):
        ks = pl.multiple_of(kb * tk, tk)
        v = v_ref[pl.ds(ks, tk), :]
        softmax_map(bufs, 0, masked, m1_ref, l1_ref)
        pv_map(bufs, 0, v, a1_ref)
        softmax_map(bufs, 1, masked, m2_ref, l2_ref)
        pv_map(bufs, 1, v, a2_ref)

    def exact_path():
        reset()
        buf_a = (sa_ref, pa_ref, ala_ref)
        buf_b = (sb_ref, pb_ref, alb_ref)
        scores(0, buf_a)

        def body(j, carry):
            scores(2 * j + 1, buf_b)
            consume(2 * j, buf_a, False)
            scores(2 * j + 2, buf_a)
            consume(2 * j + 1, buf_b, False)
            return carry

        lax.fori_loop(0, last // 2, body, 0)

        @pl.when(last % 2 == 1)
        def _():
            scores(last, buf_b)
            consume(last - 1, buf_a, False)
            consume(last, buf_b, True)

        @pl.when(last % 2 == 0)
        def _():
            consume(last, buf_a, True)

    @pl.when(qi == 0)
    def _():
        def kbody(t, c):
            kt = k_ref[pl.ds(pl.multiple_of(t * tk, tk), tk), :].astype(F32)
            k1, k2 = kt[:, :d], kt[:, d:]
            n1 = jnp.max(jnp.sum(k1 * k1, axis=1, keepdims=True), axis=0, keepdims=True)
            n2 = jnp.max(jnp.sum(k2 * k2, axis=1, keepdims=True), axis=0, keepdims=True)
            return jnp.maximum(c[0], n1), jnp.maximum(c[1], n2)

        zero11 = jnp.zeros((1, 1), F32)
        n1, n2 = lax.fori_loop(0, k_ref.shape[0] // tk, kbody, (zero11, zero11))
        kn_ref[0] = jnp.broadcast_to(jnp.sqrt(n1), (SUBLANES, LANES))
        kn_ref[1] = jnp.broadcast_to(jnp.sqrt(n2), (SUBLANES, LANES))

    def row_bound(qh, mi):
        qf = qh.astype(F32)
        qn = jnp.sqrt(jnp.sum(qf * qf, axis=1, keepdims=True))
        return jnp.broadcast_to(qn * kn_ref[mi, 0:1, 0:1], (tq, LANES))

    b1 = row_bound(q1, 0)
    b2 = row_bound(q2, 1)

    def fast_map(s, v, bound, masked, l_ref, a_ref):
        if masked:
            keep = (lax.broadcasted_iota(jnp.int32, s.shape, 1)
                    <= lax.broadcasted_iota(jnp.int32, s.shape, 0) + diag_off)
            s = jnp.where(keep, s, MASK_VALUE)
        ps = [jnp.exp2(s[:, i * LANES:(i + 1) * LANES] - bound) for i in range(nch)]
        l_ref[...] += functools.reduce(jnp.add, ps)
        p = jnp.concatenate([x.astype(BF16) for x in ps], axis=1)
        a_ref[...] += jnp.dot(p, v, preferred_element_type=F32)

    def fast_step(kb, masked):
        ks = pl.multiple_of(kb * tk, tk)
        k = k_ref[pl.ds(ks, tk), :]
        v = v_ref[pl.ds(ks, tk), :]
        s1 = lax.dot_general(q1, k[:, :d], nt, preferred_element_type=F32)
        s2 = lax.dot_general(q2, k[:, d:], nt, preferred_element_type=F32)
        fast_map(s1, v, b1, masked, l1_ref, a1_ref)
        fast_map(s2, v, b2, masked, l2_ref, a2_ref)

    reset()

    def fast_body(kb, carry):
        fast_step(kb, False)
        return carry

    lax.fori_loop(0, last, fast_body, 0)
    fast_step(last, True)

    def row_ok(l_ref):
        return jnp.sum(l_ref[...], axis=1, keepdims=True) >= MIN_NORMALISER
    healthy = jnp.where(jnp.logical_and(row_ok(l1_ref), row_ok(l2_ref)), 1.0, 0.0)
    all_healthy = jnp.min(healthy, axis=0, keepdims=True)[0, 0] > 0.5

    @pl.when(jnp.logical_not(all_healthy))
    def _():
        exact_path()

    lv = lam_ref[...]
    lam = (jnp.exp(jnp.sum(lv[0:1, :] * lv[1:2, :], axis=-1, keepdims=True))
           - jnp.exp(jnp.sum(lv[2:3, :] * lv[3:4, :], axis=-1, keepdims=True)) + lam_init)
    l1 = jnp.sum(l1_ref[...], axis=1, keepdims=True)
    l2 = jnp.sum(l2_ref[...], axis=1, keepdims=True)
    o = a1_ref[...] / l1 - lam * (a2_ref[...] / l2)
    ms = jnp.mean(o * o, axis=-1, keepdims=True)
    o = o * lax.rsqrt(ms + SUBLN_EPS) * g_ref[...] * (1.0 - lam_init)
    o_ref[...] = o.astype(o_ref.dtype)


def _attention(z, lam_vecs, subln_g, lam_init, conv_ch, attn_width):
    l = z.shape[0]
    hw = 2 * ATTN_HEAD_DIM
    heads = attn_width // hw
    tq, tk = _tile(l, 512), _tile(l, 1024)
    qoff = 3 * conv_ch // hw
    koff = qoff + heads
    voff = koff + heads
    once = pl.Buffered(1)
    return pl.pallas_call(
        functools.partial(_attn_kernel, tq=tq, tk=tk, lam_init=lam_init),
        grid=(heads, l // tq),
        in_specs=[pl.BlockSpec((4, ATTN_HEAD_DIM), lambda h, i: (0, 0)),
                  pl.BlockSpec((1, hw), lambda h, i: (0, 0)),
                  pl.BlockSpec((tq, hw), lambda h, i: (i, qoff + h)),
                  pl.BlockSpec((l, hw), lambda h, i: (0, koff + h), pipeline_mode=once),
                  pl.BlockSpec((l, hw), lambda h, i: (0, voff + h), pipeline_mode=once)],
        out_specs=pl.BlockSpec((tq, hw), lambda h, i: (i, h)),
        out_shape=jax.ShapeDtypeStruct((l, attn_width), BF16),
        scratch_shapes=[pltpu.VMEM((tq, LANES), F32), pltpu.VMEM((tq, LANES), F32), pltpu.VMEM((tq, hw), F32),
                        pltpu.VMEM((tq, LANES), F32), pltpu.VMEM((tq, LANES), F32), pltpu.VMEM((tq, hw), F32),
                        pltpu.VMEM((2, tq, tk), F32), pltpu.VMEM((2, tq, tk), F32),
                        pltpu.VMEM((2, tq, tk), BF16), pltpu.VMEM((2, tq, tk), BF16),
                        pltpu.VMEM((2, tq, LANES), F32), pltpu.VMEM((2, tq, LANES), F32),
                        pltpu.VMEM((2, SUBLANES, LANES), F32)],
        compiler_params=_params("arbitrary", "arbitrary"),
        name="diff_attn",
    )(lam_vecs, subln_g.reshape(1, hw), z, z, z)


def _s5_prep_kernel(are_ref, aim_ref, ldt_ref, bre_ref, bim_ref, bbre_ref, bbim_ref, pwre_ref, pwim_ref,
                    *, log2_sub):
    lam_re = jnp.minimum(are_ref[...], -1e-4)
    lam_im = aim_ref[...]
    dt = jnp.exp(ldt_ref[...])
    mag = jnp.exp(lam_re * dt)
    ab_re = mag * jnp.cos(lam_im * dt)
    ab_im = mag * jnp.sin(lam_im * dt)
    den = lam_re * lam_re + lam_im * lam_im
    nr, ni = ab_re - 1.0, ab_im
    f_re = (nr * lam_re + ni * lam_im) / den
    f_im = (ni * lam_re - nr * lam_im) / den
    b_re, b_im = bre_ref[...], bim_ref[...]
    bbre_ref[...] = f_re * b_re - f_im * b_im
    bbim_ref[...] = f_re * b_im + f_im * b_re
    pwre_ref[0] = ab_re
    pwim_ref[0] = ab_im
    sr, si = ab_re, ab_im
    for _ in range(log2_sub):
        sr, si = sr * sr - si * si, 2.0 * sr * si
    pr, pi = sr, si
    for k in range(1, SUBLANES + 1):
        pwre_ref[k] = pr
        pwim_ref[k] = pi
        pr, pi = pr * sr - pi * si, pr * si + pi * sr


def _s5_prepare(a_re, a_im, log_dt, b_re, b_im, c_re, c_im, sub_rows):
    g, p = a_re.shape
    h = S5_GROUP
    w = p * h
    rep = lambda t: jnp.repeat(t, h, axis=1)
    tg = _tile(g, 64)
    spec = pl.BlockSpec((tg, w), lambda i: (i, 0))
    npw = SUBLANES + 1
    spec3 = pl.BlockSpec((npw, tg, w), lambda i: (0, i, 0))
    log2_sub = sub_rows.bit_length() - 1
    assert 1 << log2_sub == sub_rows
    bb_re, bb_im, pw_re, pw_im = pl.pallas_call(
        functools.partial(_s5_prep_kernel, log2_sub=log2_sub),
        grid=(g // tg,),
        in_specs=[spec] * 5,
        out_specs=[spec, spec, spec3, spec3],
        out_shape=[jax.ShapeDtypeStruct((g, w), F32)] * 2 + [jax.ShapeDtypeStruct((npw, g, w), F32)] * 2,
        compiler_params=_params("arbitrary"),
        name="s5_prep",
    )(rep(a_re), rep(a_im), jnp.broadcast_to(log_dt[:, None], (g, w)),
      b_re.reshape(g, w), b_im.reshape(g, w))

    nb = g // GROUPS_PER_BLOCK
    eye = jnp.eye(GROUPS_PER_BLOCK, dtype=F32)

    def in_mat(bb):
        t = bb.reshape(nb, GROUPS_PER_BLOCK, p, h).transpose(0, 1, 3, 2)
        t = t[:, :, :, None, :] * eye[None, :, None, :, None]
        return t.reshape(nb, CH_BLOCK, ST_BLOCK)

    def out_mat(c):
        t = c.reshape(nb, GROUPS_PER_BLOCK, h, p).transpose(0, 1, 3, 2)
        t = t[:, :, :, None, :] * eye[None, :, None, :, None]
        return t.reshape(nb, ST_BLOCK, CH_BLOCK)

    b_mat = jnp.concatenate([in_mat(bb_re), in_mat(bb_im)], axis=2).astype(BF16)
    c_mat = jnp.concatenate([out_mat(c_re), -out_mat(c_im)], axis=1).astype(BF16)

    def powers(pw):
        return pw[:, :, ::h].reshape(npw, nb, ST_BLOCK).transpose(1, 0, 2)

    pr, pi = powers(pw_re), powers(pw_im)
    ones = jnp.ones((1, SUBLANES, 1), F32)
    t_idx = jnp.arange(SUBLANES)[None, :, None]
    tabs = [pr[:, 0:1, :] * ones, pi[:, 0:1, :] * ones]
    for dshift in (1, 2, 4):
        keep = (t_idx >= dshift).astype(F32)
        tabs += [pr[:, dshift:dshift + 1, :] * keep, pi[:, dshift:dshift + 1, :] * keep]
    tabs += [pr[:, 1:, :], pi[:, 1:, :]]
    return b_mat, c_mat, jnp.stack(tabs, axis=1)


def _s5_kernel(u_ref, pm_ref, pmt_ref, b_ref, c_ref, t_ref, d_ref, o_ref, st_ref, carry_ref, *, tt):
    ci = pl.program_id(1)
    nlb = ST_BLOCK // LANES
    sub_rows = tt // SUBLANES

    @pl.when(ci == 0)
    def _():
        carry_ref[...] = jnp.zeros(carry_ref.shape, F32)

    def lanes(j):
        return (slice(j * LANES, (j + 1) * LANES),
                slice(ST_BLOCK + j * LANES, ST_BLOCK + (j + 1) * LANES))

    def project_in(h):
        ch = slice(h * CH_BLOCK, (h + 1) * CH_BLOCK)
        u = jnp.dot(pm_ref[...], u_ref[:, ch], preferred_element_type=F32).astype(BF16)
        st_ref[h] = jnp.dot(u, b_ref[h], preferred_element_type=F32)
        return u

    def scan(h, init, store):
        def step(r, x):
            row = slice(r * SUBLANES, (r + 1) * SUBLANES)
            out = []
            for j in range(nlb):
                lo, hi = lanes(j)
                ar, ai = t_ref[h, 0, :, lo], t_ref[h, 1, :, lo]
                xr, xi = x[2 * j], x[2 * j + 1]
                nr = ar * xr - ai * xi + st_ref[h, row, lo]
                ni = ar * xi + ai * xr + st_ref[h, row, hi]
                if store:
                    st_ref[h, row, lo] = nr
                    st_ref[h, row, hi] = ni
                out += [nr, ni]
            return tuple(out)

        x = init
        for r in range(sub_rows):
            x = step(r, x)
        return x

    def start_states(h, ends):
        first = lax.broadcasted_iota(jnp.int32, (SUBLANES, LANES), 0) == 0
        top = SUBLANES - 1
        init = []
        for j in range(nlb):
            lo, _ = lanes(j)
            yr, yi = ends[2 * j], ends[2 * j + 1]
            for n, dshift in enumerate((1, 2, 4)):
                ar, ai = t_ref[h, 2 + 2 * n, :, lo], t_ref[h, 3 + 2 * n, :, lo]
                rr = pltpu.roll(yr, dshift, 0)
                ri = pltpu.roll(yi, dshift, 0)
                yr, yi = yr + ar * rr - ai * ri, yi + ar * ri + ai * rr
            pr, pi = t_ref[h, 8, :, lo], t_ref[h, 9, :, lo]
            cr, cim = carry_ref[h, 0, :, lo], carry_ref[h, 1, :, lo]
            er = yr + pr * cr - pi * cim
            ei = yi + pr * cim + pi * cr
            init += [jnp.where(first, cr, pltpu.roll(er, 1, 0)),
                     jnp.where(first, cim, pltpu.roll(ei, 1, 0))]
            carry_ref[h, 0, :, lo] = jnp.broadcast_to(er[top:top + 1, :], er.shape)
            carry_ref[h, 1, :, lo] = jnp.broadcast_to(ei[top:top + 1, :], ei.shape)
        return tuple(init)

    def project_out(h, u):
        ch = slice(h * CH_BLOCK, (h + 1) * CH_BLOCK)
        y = jnp.dot(st_ref[h].astype(BF16), c_ref[h], preferred_element_type=F32)
        y = y + d_ref[:, ch] * u.astype(F32)
        g = jax.nn.gelu(y).astype(BF16)
        o_ref[:, ch] = jnp.dot(pmt_ref[...], g, preferred_element_type=F32).astype(o_ref.dtype)

    zero = (jnp.zeros((SUBLANES, LANES), F32),) * (2 * nlb)
    us = [project_in(h) for h in range(BLOCKS_PER_STEP)]
    for h in range(BLOCKS_PER_STEP):
        ends = scan(h, zero, False)
        scan(h, start_states(h, ends), True)
        project_out(h, us[h])


def _s5_tile(l):
    return _tile(l, 512)


def _s5(u, b_mat, c_mat, tabs, d_skip):
    l, d = u.shape
    nb = d // CH_BLOCK
    bps = BLOCKS_PER_STEP
    assert nb % bps == 0
    tt = _s5_tile(l)
    rho = jnp.arange(tt)
    pos = (rho % SUBLANES) * (tt // SUBLANES) + rho // SUBLANES
    pm = (pos[:, None] == jnp.arange(tt)[None, :]).astype(BF16)
    return pl.pallas_call(
        functools.partial(_s5_kernel, tt=tt),
        grid=(nb // bps, l // tt),
        in_specs=[pl.BlockSpec((tt, bps * CH_BLOCK), lambda b, i: (i, b)),
                  pl.BlockSpec((tt, tt), lambda b, i: (0, 0)),
                  pl.BlockSpec((tt, tt), lambda b, i: (0, 0)),
                  pl.BlockSpec((bps, CH_BLOCK, 2 * ST_BLOCK), lambda b, i: (b, 0, 0)),
                  pl.BlockSpec((bps, 2 * ST_BLOCK, CH_BLOCK), lambda b, i: (b, 0, 0)),
                  pl.BlockSpec((bps, 10, SUBLANES, ST_BLOCK), lambda b, i: (b, 0, 0, 0)),
                  pl.BlockSpec((1, bps * CH_BLOCK), lambda b, i: (0, b))],
        out_specs=pl.BlockSpec((tt, bps * CH_BLOCK), lambda b, i: (i, b)),
        out_shape=jax.ShapeDtypeStruct((l, d), BF16),
        scratch_shapes=[pltpu.VMEM((bps, tt, 2 * ST_BLOCK), F32),
                        pltpu.VMEM((bps, 2, SUBLANES, ST_BLOCK), F32)],
        compiler_params=_params("arbitrary", "arbitrary"),
        name="s5_scan",
    )(u, pm, pm.T, b_mat, c_mat, tabs, d_skip.reshape(1, d))


def kernel(x, c, w_ada, b_ada, ada_table, norm_mix, norm_ffn, norm_final, mix_w_in, conv_w, lambda_q1, lambda_k1, lambda_q2, lambda_k2, subln_g, mix_w_out, s5_a_re, s5_a_im, s5_log_dt, s5_b_re, s5_b_im, s5_c_re, s5_c_im, s5_d, glu_w1, glu_w2, ffn_w_gate, ffn_w_up, ffn_w_down):
    bsz, seq, d = x.shape
    assert bsz == 1
    depth = ada_table.shape[0]
    conv_ch = conv_w.shape[2]
    attn_width = mix_w_out.shape[1] - conv_ch
    hidden = ffn_w_gate.shape[2]
    hidden_pad = _round_up(hidden, 512)
    tk_down = hidden_pad // 4
    assert tk_down % LANES == 0 and hidden_pad - hidden < min(512, tk_down)

    w_in = mix_w_in.astype(BF16)
    w_out = mix_w_out.astype(BF16)
    w_glu1 = glu_w1.astype(BF16)
    w_glu2 = glu_w2.astype(BF16)
    w_gate = ffn_w_gate.astype(BF16)
    w_up = ffn_w_up.astype(BF16)
    w_down = ffn_w_down.astype(BF16)

    mods = _ada(c, w_ada, b_ada, ada_table)
    xs = x.reshape(seq, d)
    for l in range(depth):
        r = l * N_MOD
        h = _norm_mod(xs, norm_mix[l], mods, r + 0, r + 1)
        if l % 2 == 0:
            e = l // 2
            lam_init = 0.8 - 0.6 * math.exp(-0.3 * l)
            z = _matmul(h, w_in, e)
            y_conv = _conv(z, conv_w[e])
            lam_vecs = jnp.stack([lambda_q1[e], lambda_k1[e], lambda_q2[e], lambda_k2[e]])
            y_attn = _attention(z, lam_vecs, subln_g[e], lam_init, conv_ch, attn_width)
            xs = _mm2_res(y_conv, y_attn, w_out, e, xs, mods, r + 2)
        else:
            o = l // 2
            b_mat, c_mat, tabs = _s5_prepare(s5_a_re[o], s5_a_im[o], s5_log_dt[o],
                                             s5_b_re[o], s5_b_im[o], s5_c_re[o], s5_c_im[o],
                                             _s5_tile(seq) // SUBLANES)
            g = _s5(h, b_mat, c_mat, tabs, s5_d[o])
            xs = _glu_res(g, w_glu1, w_glu2, o, xs, mods, r + 2)
        h = _norm_mod(xs, norm_ffn[l], mods, r + 3, r + 4)
        act = _swiglu_up(h, w_gate, w_up, l, hidden_pad)
        xs = _mm_res(act, w_down, l, xs, mods, r + 5, tk_down)
    return _final_norm(xs, norm_final).reshape(bsz, seq, d)
```

```python
import functools
import math

import jax
import jax.numpy as jnp
from jax import lax
from jax.experimental import pallas as pl
from jax.experimental.pallas import tpu as pltpu

F32 = jnp.float32
BF16 = jnp.bfloat16

N_MOD = 6
CONV_K = 3
ATTN_HEAD_DIM = 128
S5_GROUP = 16
S5_STATE = 64
NORM_EPS = 1e-6
SUBLN_EPS = 1e-5

GROUPS_PER_BLOCK = 16
CH_BLOCK = GROUPS_PER_BLOCK * S5_GROUP
ST_BLOCK = GROUPS_PER_BLOCK * S5_STATE
BLOCKS_PER_STEP = 2
SUBLANES = 8
LANES = 128
MASK_VALUE = -1e30
MIN_NORMALISER = 2.0 ** -80
VMEM_LIMIT = 56 * 1024 * 1024


def _params(*sem):
    return pltpu.CompilerParams(dimension_semantics=sem, vmem_limit_bytes=VMEM_LIMIT)


def _tile(dim, pref):
    t = min(pref, dim)
    while dim % t:
        t //= 2
    return t


def _round_up(n, m):
    return -(-n // m) * m


def _ada_kernel(c_ref, w_ref, b_ref, t_ref, o_ref):
    c = c_ref[...]
    a = (c * jax.nn.sigmoid(c)).astype(BF16)
    r = jnp.dot(a, w_ref[...].astype(BF16), preferred_element_type=F32)
    o_ref[...] = r[0:1, :] + b_ref[...] + t_ref[...]


def _ada(c, w_ada, b_ada, ada_table):
    d = c.shape[1]
    depth = ada_table.shape[0]
    n = w_ada.shape[1]
    tn = _tile(n, 512)
    c8 = jnp.broadcast_to(c, (SUBLANES, d))
    out = pl.pallas_call(
        _ada_kernel,
        grid=(n // tn,),
        in_specs=[pl.BlockSpec((SUBLANES, d), lambda j: (0, 0)),
                  pl.BlockSpec((d, tn), lambda j: (0, j)),
                  pl.BlockSpec((1, tn), lambda j: (0, j)),
                  pl.BlockSpec((depth, tn), lambda j: (0, j))],
        out_specs=pl.BlockSpec((depth, tn), lambda j: (0, j)),
        out_shape=jax.ShapeDtypeStruct((depth, n), F32),
        compiler_params=_params("arbitrary"),
        name="ada_proj",
    )(c8, w_ada, b_ada.reshape(1, n), ada_table.reshape(depth, n))
    return out.reshape(depth * N_MOD, d)


def _norm_mod_kernel(x_ref, g_ref, m_ref, *rest, shift_row, scale_row):
    o_ref = rest[-1]
    x = x_ref[...]
    ms = jnp.mean(x * x, axis=-1, keepdims=True)
    y = x * lax.rsqrt(ms + NORM_EPS) * g_ref[...]
    shift = m_ref[shift_row:shift_row + 1, :]
    scale = m_ref[scale_row:scale_row + 1, :]
    h = (y * (1.0 + scale) + shift).astype(o_ref.dtype)
    if len(rest) == 2:
        h = jnp.dot(rest[0][...], h, preferred_element_type=F32).astype(o_ref.dtype)
    o_ref[...] = h


def _norm_mod(x, g, mods, shift_row, scale_row, row_perm=None):
    l, d = x.shape
    tm = _tile(l, 512)
    extra_specs, extra = [], []
    if row_perm is not None:
        assert row_perm.shape == (tm, tm)
        extra_specs, extra = [pl.BlockSpec((tm, tm), lambda i: (0, 0))], [row_perm]
    return pl.pallas_call(
        functools.partial(_norm_mod_kernel, shift_row=shift_row, scale_row=scale_row),
        grid=(l // tm,),
        in_specs=[pl.BlockSpec((tm, d), lambda i: (i, 0)),
                  pl.BlockSpec((1, d), lambda i: (0, 0)),
                  pl.BlockSpec(mods.shape, lambda i: (0, 0))] + extra_specs,
        out_specs=pl.BlockSpec((tm, d), lambda i: (i, 0)),
        out_shape=jax.ShapeDtypeStruct((l, d), BF16),
        compiler_params=_params("arbitrary"),
        name="norm_mod",
    )(x, g.reshape(1, d), mods, *extra)


def _norm_kernel(x_ref, g_ref, o_ref):
    x = x_ref[...]
    ms = jnp.mean(x * x, axis=-1, keepdims=True)
    o_ref[...] = x * lax.rsqrt(ms + NORM_EPS) * g_ref[...]


def _final_norm(x, g):
    l, d = x.shape
    tm = _tile(l, 512)
    return pl.pallas_call(
        _norm_kernel,
        grid=(l // tm,),
        in_specs=[pl.BlockSpec((tm, d), lambda i: (i, 0)),
                  pl.BlockSpec((1, d), lambda i: (0, 0))],
        out_specs=pl.BlockSpec((tm, d), lambda i: (i, 0)),
        out_shape=jax.ShapeDtypeStruct((l, d), F32),
        compiler_params=_params("arbitrary"),
        name="final_norm",
    )(x, g.reshape(1, d))


def _mm_kernel(a_ref, w_ref, o_ref):
    o_ref[...] = jnp.dot(a_ref[...], w_ref[...], preferred_element_type=F32).astype(o_ref.dtype)


def _matmul(a, w, layer):
    m, k = a.shape
    n = w.shape[2]
    tm, tn = _tile(m, 1024), _tile(n, 1024)
    return pl.pallas_call(
        _mm_kernel,
        grid=(m // tm, n // tn),
        in_specs=[pl.BlockSpec((tm, k), lambda i, j: (i, 0)),
                  pl.BlockSpec((None, k, tn), lambda i, j: (layer, 0, j))],
        out_specs=pl.BlockSpec((tm, tn), lambda i, j: (i, j)),
        out_shape=jax.ShapeDtypeStruct((m, n), BF16),
        compiler_params=_params("arbitrary", "arbitrary"),
        name="in_proj",
    )(a, w)


def _swiglu_up_kernel(a_ref, wg_ref, wu_ref, o_ref, *, hidden):
    a = a_ref[...]
    g = jnp.dot(a, wg_ref[...], preferred_element_type=F32)
    u = jnp.dot(a, wu_ref[...], preferred_element_type=F32)
    act = g * jax.nn.sigmoid(g) * u
    tn = act.shape[1]
    col = pl.program_id(1) * tn + lax.broadcasted_iota(jnp.int32, (1, tn), 1)
    o_ref[...] = jnp.where(col < hidden, act, 0.0).astype(o_ref.dtype)


def _swiglu_up(a, wg, wu, layer, n_pad):
    m, k = a.shape
    hidden = wg.shape[2]
    tm, tn = _tile(m, 1024), _tile(n_pad, 512)
    return pl.pallas_call(
        functools.partial(_swiglu_up_kernel, hidden=hidden),
        grid=(m // tm, n_pad // tn),
        in_specs=[pl.BlockSpec((tm, k), lambda i, j: (i, 0)),
                  pl.BlockSpec((None, k, tn), lambda i, j: (layer, 0, j)),
                  pl.BlockSpec((None, k, tn), lambda i, j: (layer, 0, j))],
        out_specs=pl.BlockSpec((tm, tn), lambda i, j: (i, j)),
        out_shape=jax.ShapeDtypeStruct((m, n_pad), BF16),
        compiler_params=_params("arbitrary", "arbitrary"),
        name="ffn_up",
    )(a, wg, wu)


def _glu_res_kernel(a_ref, w1_ref, w2_ref, x_ref, m_ref, o_ref, *, gate_row):
    a = a_ref[...]
    y1 = jnp.dot(a, w1_ref[...], preferred_element_type=F32)
    y2 = jnp.dot(a, w2_ref[...], preferred_element_type=F32)
    gate = m_ref[gate_row:gate_row + 1, :]
    o_ref[...] = x_ref[...] + gate * (y1 * jax.nn.sigmoid(y2))


def _glu_res(a, w1, w2, layer, x, mods, gate_row):
    m, k = a.shape
    n = w1.shape[2]
    tm, tn = _tile(m, 1024), _tile(n, 512)
    return pl.pallas_call(
        functools.partial(_glu_res_kernel, gate_row=gate_row),
        grid=(m // tm, n // tn),
        in_specs=[pl.BlockSpec((tm, k), lambda i, j: (i, 0)),
                  pl.BlockSpec((None, k, tn), lambda i, j: (layer, 0, j)),
                  pl.BlockSpec((None, k, tn), lambda i, j: (layer, 0, j)),
                  pl.BlockSpec((tm, tn), lambda i, j: (i, j)),
                  pl.BlockSpec((mods.shape[0], tn), lambda i, j: (0, j))],
        out_specs=pl.BlockSpec((tm, tn), lambda i, j: (i, j)),
        out_shape=jax.ShapeDtypeStruct((m, n), F32),
        compiler_params=_params("arbitrary", "arbitrary"),
        name="glu_res",
    )(a, w1, w2, x, mods)


def _mm_res_kernel(a_ref, w_ref, x_ref, m_ref, o_ref, acc_ref, *, gate_row, nk, k_valid):
    kk = pl.program_id(2)
    w = w_ref[...]
    tk = w.shape[0]
    row = kk * tk + lax.broadcasted_iota(jnp.int32, (tk, 1), 0)
    w = jnp.where(row < k_valid, w, jnp.zeros_like(w))
    part = jnp.dot(a_ref[...], w, preferred_element_type=F32)

    @pl.when(kk == 0)
    def _():
        acc_ref[...] = part

    @pl.when(kk > 0)
    def _():
        acc_ref[...] += part

    @pl.when(kk == nk - 1)
    def _():
        gate = m_ref[gate_row:gate_row + 1, :]
        o_ref[...] = x_ref[...] + gate * acc_ref[...]


def _mm_res(a, w, layer, x, mods, gate_row, tk):
    m, k = a.shape
    n = w.shape[2]
    tm, tn = _tile(m, 1024), _tile(n, 1024)
    nk = k // tk
    return pl.pallas_call(
        functools.partial(_mm_res_kernel, gate_row=gate_row, nk=nk, k_valid=w.shape[1]),
        grid=(m // tm, n // tn, nk),
        in_specs=[pl.BlockSpec((tm, tk), lambda i, j, kk: (i, kk)),
                  pl.BlockSpec((None, tk, tn), lambda i, j, kk: (layer, kk, j)),
                  pl.BlockSpec((tm, tn), lambda i, j, kk: (i, j)),
                  pl.BlockSpec((mods.shape[0], tn), lambda i, j, kk: (0, j))],
        out_specs=pl.BlockSpec((tm, tn), lambda i, j, kk: (i, j)),
        out_shape=jax.ShapeDtypeStruct((m, n), F32),
        scratch_shapes=[pltpu.VMEM((tm, tn), F32)],
        compiler_params=_params("arbitrary", "arbitrary", "arbitrary"),
        name="ffn_down",
    )(a, w, x, mods)


def _mm2_res_kernel(a1_ref, a2_ref, w1_ref, w2_ref, x_ref, m_ref, o_ref, *, gate_row):
    y = jnp.dot(a1_ref[...], w1_ref[...], preferred_element_type=F32)
    y = y + jnp.dot(a2_ref[...], w2_ref[...], preferred_element_type=F32)
    gate = m_ref[gate_row:gate_row + 1, :]
    o_ref[...] = x_ref[...] + gate * y


def _mm2_res(a1, a2, w, layer, x, mods, gate_row):
    m, k1 = a1.shape
    k2 = a2.shape[1]
    assert k1 == k2 and w.shape[1] == k1 + k2
    n = w.shape[2]
    tm, tn = _tile(m, 1024), _tile(n, 512)
    return pl.pallas_call(
        functools.partial(_mm2_res_kernel, gate_row=gate_row),
        grid=(m // tm, n // tn),
        in_specs=[pl.BlockSpec((tm, k1), lambda i, j: (i, 0)),
                  pl.BlockSpec((tm, k2), lambda i, j: (i, 0)),
                  pl.BlockSpec((None, k1, tn), lambda i, j: (layer, 0, j)),
                  pl.BlockSpec((None, k2, tn), lambda i, j: (layer, 1, j)),
                  pl.BlockSpec((tm, tn), lambda i, j: (i, j)),
                  pl.BlockSpec((mods.shape[0], tn), lambda i, j: (0, j))],
        out_specs=pl.BlockSpec((tm, tn), lambda i, j: (i, j)),
        out_shape=jax.ShapeDtypeStruct((m, n), F32),
        compiler_params=_params("arbitrary", "arbitrary"),
        name="out_proj",
    )(a1, a2, w, w, x, mods)


def _conv_kernel(b_ref, c_ref, xi_ref, ch_ref, xh_ref, w_ref, o_ref):
    i = pl.program_id(0)
    p = c_ref[...].astype(F32) * xi_ref[...].astype(F32)
    halo = ch_ref[...].astype(F32) * xh_ref[...].astype(F32)
    halo = halo * jnp.where(i > 0, 1.0, 0.0)
    nh = halo.shape[0]
    prev1 = halo[nh - 1:nh, :]
    prev2 = halo[nh - 2:nh - 1, :]
    row = lax.broadcasted_iota(jnp.int32, p.shape, 0)
    p1 = jnp.where(row == 0, prev1, pltpu.roll(p, 1, 0))
    p2 = jnp.where(row == 0, prev2, jnp.where(row == 1, prev1, pltpu.roll(p, 2, 0)))
    w = w_ref[...]
    y = w[2:3, :] * p + w[1:2, :] * p1 + w[0:1, :] * p2
    o_ref[...] = (b_ref[...].astype(F32) * y).astype(o_ref.dtype)


def _conv(z, conv_w):
    l = z.shape[0]
    ch = conv_w.shape[1]
    tm, tc = _tile(l, 512), _tile(ch, 512)
    halo = 16
    nc = ch // tc
    rb = tm // halo
    return pl.pallas_call(
        _conv_kernel,
        grid=(l // tm, nc),
        in_specs=[pl.BlockSpec((tm, tc), lambda i, j: (i, j)),
                  pl.BlockSpec((tm, tc), lambda i, j: (i, nc + j)),
                  pl.BlockSpec((tm, tc), lambda i, j: (i, 2 * nc + j)),
                  pl.BlockSpec((halo, tc), lambda i, j: (jnp.maximum(i * rb - 1, 0), nc + j)),
                  pl.BlockSpec((halo, tc), lambda i, j: (jnp.maximum(i * rb - 1, 0), 2 * nc + j)),
                  pl.BlockSpec((CONV_K, tc), lambda i, j: (0, j))],
        out_specs=pl.BlockSpec((tm, tc), lambda i, j: (i, j)),
        out_shape=jax.ShapeDtypeStruct((l, ch), BF16),
        compiler_params=_params("arbitrary", "arbitrary"),
        name="gated_conv",
    )(z, z, z, z, z, conv_w)


def _attn_kernel(lam_ref, g_ref, q_ref, k_ref, v_ref, o_ref,
                 m1_ref, l1_ref, a1_ref, m2_ref, l2_ref, a2_ref,
                 sa_ref, sb_ref, pa_ref, pb_ref, ala_ref, alb_ref, kn_ref, *, tq, tk, lam_init):
    qi = pl.program_id(1)
    d = ATTN_HEAD_DIM
    q = q_ref[...].astype(F32) * (d ** -0.5 * math.log2(math.e))
    q1 = q[:, :d].astype(BF16)
    q2 = q[:, d:].astype(BF16)

    def reset():
        m1_ref[...] = jnp.full(m1_ref.shape, MASK_VALUE, F32)
        m2_ref[...] = jnp.full(m2_ref.shape, MASK_VALUE, F32)
        l1_ref[...] = jnp.zeros(l1_ref.shape, F32)
        l2_ref[...] = jnp.zeros(l2_ref.shape, F32)
        a1_ref[...] = jnp.zeros(a1_ref.shape, F32)
        a2_ref[...] = jnp.zeros(a2_ref.shape, F32)

    nt = (((1,), (1,)), ((), ()))
    nch = tk // LANES
    rc = min(tq, 64)
    last = (qi * tq) // tk
    diag_off = qi * tq - last * tk

    def softmax_map(bufs, mi, masked, m_ref, l_ref):
        s_ref, p_ref, al_ref = bufs
        for r0 in range(0, tq, rc):
            rows = slice(r0, r0 + rc)
            s = s_ref[mi, rows, :]
            if masked:
                keep = (lax.broadcasted_iota(jnp.int32, s.shape, 1)
                        <= lax.broadcasted_iota(jnp.int32, s.shape, 0) + (r0 + diag_off))
                s = jnp.where(keep, s, MASK_VALUE)
            cols = [s[:, i * LANES:(i + 1) * LANES] for i in range(nch)]
            mx = functools.reduce(jnp.maximum, cols)
            m_old = m_ref[rows, :]
            m_new = jnp.maximum(m_old, jnp.max(mx, axis=1, keepdims=True))
            alpha = jnp.exp2(m_old - m_new)
            ps = [jnp.exp2(c - m_new) for c in cols]
            l_ref[rows, :] = alpha * l_ref[rows, :] + functools.reduce(jnp.add, ps)
            p_ref[mi, rows, :] = jnp.concatenate([x.astype(BF16) for x in ps], axis=1)
            al_ref[mi, rows, :] = alpha
            m_ref[rows, :] = m_new

    def pv_map(bufs, mi, v, a_ref):
        _, p_ref, al_ref = bufs
        pv = jnp.dot(p_ref[mi], v, preferred_element_type=F32)
        alpha = al_ref[mi]
        a_ref[...] = jnp.concatenate([alpha, alpha], axis=1) * a_ref[...] + pv

    def scores(kb, bufs):
        s_ref = bufs[0]
        ks = pl.multiple_of(kb * tk, tk)
        k = k_ref[pl.ds(ks, tk), :]
        s_ref[0] = lax.dot_general(q1, k[:, :d], nt, preferred_element_type=F32)
        s_ref[1] = lax.dot_general(q2, k[:, d:], nt, preferred_element_type=F32)

    def consume(kb, bufs, masked):
        ks = pl.multiple_of(kb * tk, tk)
        v = v_ref[pl.ds(ks, tk), :]
        softmax_map(bufs, 0, masked, m1_ref, l1_ref)
        pv_map(bufs, 0, v, a1_ref)
        softmax_map(bufs, 1, masked, m2_ref, l2_ref)
        pv_map(bufs, 1, v, a2_ref)

    def exact_path():
        reset()
        buf_a = (sa_ref, pa_ref, ala_ref)
        buf_b = (sb_ref, pb_ref, alb_ref)
        scores(0, buf_a)

        def body(j, carry):
            scores(2 * j + 1, buf_b)
            consume(2 * j, buf_a, False)
            scores(2 * j + 2, buf_a)
            consume(2 * j + 1, buf_b, False)
            return carry

        lax.fori_loop(0, last // 2, body, 0)

        @pl.when(last % 2 == 1)
        def _():
            scores(last, buf_b)
            consume(last - 1, buf_a, False)
            consume(last, buf_b, True)

        @pl.when(last % 2 == 0)
        def _():
            consume(last, buf_a, True)

    @pl.when(qi == 0)
    def _():
        def kbody(t, c):
            kt = k_ref[pl.ds(pl.multiple_of(t * tk, tk), tk), :].astype(F32)
            k1, k2 = kt[:, :d], kt[:, d:]
            n1 = jnp.max(jnp.sum(k1 * k1, axis=1, keepdims=True), axis=0, keepdims=True)
            n2 = jnp.max(jnp.sum(k2 * k2, axis=1, keepdims=True), axis=0, keepdims=True)
            return jnp.maximum(c[0], n1), jnp.maximum(c[1], n2)

        zero11 = jnp.zeros((1, 1), F32)
        n1, n2 = lax.fori_loop(0, k_ref.shape[0] // tk, kbody, (zero11, zero11))
        kn_ref[0] = jnp.broadcast_to(jnp.sqrt(n1), (SUBLANES, LANES))
        kn_ref[1] = jnp.broadcast_to(jnp.sqrt(n2), (SUBLANES, LANES))

    def row_bound(qh, mi):
        qf = qh.astype(F32)
        qn = jnp.sqrt(jnp.sum(qf * qf, axis=1, keepdims=True))
        return jnp.broadcast_to(qn * kn_ref[mi, 0:1, 0:1], (tq, LANES))

    b1 = row_bound(q1, 0)
    b2 = row_bound(q2, 1)

    def fast_map(s, v, bound, masked, l_ref, a_ref):
        if masked:
            keep = (lax.broadcasted_iota(jnp.int32, s.shape, 1)
                    <= lax.broadcasted_iota(jnp.int32, s.shape, 0))
            s = jnp.where(keep, s, MASK_VALUE)
        ps = [jnp.exp2(s[:, i * LANES:(i + 1) * LANES] - bound) for i in range(s.shape[1] // LANES)]
        l_ref[...] += functools.reduce(jnp.add, ps)
        p = jnp.concatenate([x.astype(BF16) for x in ps], axis=1)
        a_ref[...] += jnp.dot(p, v, preferred_element_type=F32)

    def fast_step(ks, width, masked):
        ks = pl.multiple_of(ks, width)
        k = k_ref[pl.ds(ks, width), :]
        v = v_ref[pl.ds(ks, width), :]
        s1 = lax.dot_general(q1, k[:, :d], nt, preferred_element_type=F32)
        s2 = lax.dot_general(q2, k[:, d:], nt, preferred_element_type=F32)
        fast_map(s1, v, b1, masked, l1_ref, a1_ref)
        fast_map(s2, v, b2, masked, l2_ref, a2_ref)

    reset()

    def fast_body(j, carry):
        fast_step(2 * j * tk, tk, False)
        fast_step((2 * j + 1) * tk, tk, False)
        return carry

    lax.fori_loop(0, last // 2, fast_body, 0)

    @pl.when(last % 2 == 1)
    def _():
        fast_step((last - 1) * tk, tk, False)

    @pl.when(diag_off > 0)
    def _():
        fast_step(last * tk, tq, False)

    fast_step(qi * tq, tq, True)

    def row_ok(l_ref):
        return jnp.sum(l_ref[...], axis=1, keepdims=True) >= MIN_NORMALISER
    healthy = jnp.where(jnp.logical_and(row_ok(l1_ref), row_ok(l2_ref)), 1.0, 0.0)
    all_healthy = jnp.min(healthy, axis=0, keepdims=True)[0, 0] > 0.5

    @pl.when(jnp.logical_not(all_healthy))
    def _():
        exact_path()

    lv = lam_ref[...]
    lam = (jnp.exp(jnp.sum(lv[0:1, :] * lv[1:2, :], axis=-1, keepdims=True))
           - jnp.exp(jnp.sum(lv[2:3, :] * lv[3:4, :], axis=-1, keepdims=True)) + lam_init)
    l1 = jnp.sum(l1_ref[...], axis=1, keepdims=True)
    l2 = jnp.sum(l2_ref[...], axis=1, keepdims=True)
    o = a1_ref[...] / l1 - lam * (a2_ref[...] / l2)
    ms = jnp.mean(o * o, axis=-1, keepdims=True)
    o = o * lax.rsqrt(ms + SUBLN_EPS) * g_ref[...] * (1.0 - lam_init)
    o_ref[...] = o.astype(o_ref.dtype)


def _attention(z, lam_vecs, subln_g, lam_init, conv_ch, attn_width):
    l = z.shape[0]
    hw = 2 * ATTN_HEAD_DIM
    heads = attn_width // hw
    tq, tk = _tile(l, 512), _tile(l, 1024)
    assert tk in (tq, 2 * tq)
    qoff = 3 * conv_ch // hw
    koff = qoff + heads
    voff = koff + heads
    once = pl.Buffered(1)
    return pl.pallas_call(
        functools.partial(_attn_kernel, tq=tq, tk=tk, lam_init=lam_init),
        grid=(heads, l // tq),
        in_specs=[pl.BlockSpec((4, ATTN_HEAD_DIM), lambda h, i: (0, 0)),
                  pl.BlockSpec((1, hw), lambda h, i: (0, 0)),
                  pl.BlockSpec((tq, hw), lambda h, i: (i, qoff + h)),
                  pl.BlockSpec((l, hw), lambda h, i: (0, koff + h), pipeline_mode=once),
                  pl.BlockSpec((l, hw), lambda h, i: (0, voff + h), pipeline_mode=once)],
        out_specs=pl.BlockSpec((tq, hw), lambda h, i: (i, h)),
        out_shape=jax.ShapeDtypeStruct((l, attn_width), BF16),
        scratch_shapes=[pltpu.VMEM((tq, LANES), F32), pltpu.VMEM((tq, LANES), F32), pltpu.VMEM((tq, hw), F32),
                        pltpu.VMEM((tq, LANES), F32), pltpu.VMEM((tq, LANES), F32), pltpu.VMEM((tq, hw), F32),
                        pltpu.VMEM((2, tq, tk), F32), pltpu.VMEM((2, tq, tk), F32),
                        pltpu.VMEM((2, tq, tk), BF16), pltpu.VMEM((2, tq, tk), BF16),
                        pltpu.VMEM((2, tq, LANES), F32), pltpu.VMEM((2, tq, LANES), F32),
                        pltpu.VMEM((2, SUBLANES, LANES), F32)],
        compiler_params=_params("arbitrary", "arbitrary"),
        name="diff_attn",
    )(lam_vecs, subln_g.reshape(1, hw), z, z, z)


def _s5_prep_kernel(are_ref, aim_ref, ldt_ref, bre_ref, bim_ref, bbre_ref, bbim_ref, pwre_ref, pwim_ref,
                    *, log2_sub):
    lam_re = jnp.minimum(are_ref[...], -1e-4)
    lam_im = aim_ref[...]
    dt = jnp.exp(ldt_ref[...])
    mag = jnp.exp(lam_re * dt)
    ab_re = mag * jnp.cos(lam_im * dt)
    ab_im = mag * jnp.sin(lam_im * dt)
    den = lam_re * lam_re + lam_im * lam_im
    nr, ni = ab_re - 1.0, ab_im
    f_re = (nr * lam_re + ni * lam_im) / den
    f_im = (ni * lam_re - nr * lam_im) / den
    b_re, b_im = bre_ref[...], bim_ref[...]
    bbre_ref[...] = f_re * b_re - f_im * b_im
    bbim_ref[...] = f_re * b_im + f_im * b_re
    pwre_ref[0] = ab_re
    pwim_ref[0] = ab_im
    sr, si = ab_re, ab_im
    for _ in range(log2_sub):
        sr, si = sr * sr - si * si, 2.0 * sr * si
    pr, pi = sr, si
    for k in range(1, SUBLANES + 1):
        pwre_ref[k] = pr
        pwim_ref[k] = pi
        pr, pi = pr * sr - pi * si, pr * si + pi * sr


def _s5_prepare(a_re, a_im, log_dt, b_re, b_im, c_re, c_im, sub_rows):
    g, p = a_re.shape
    h = S5_GROUP
    w = p * h
    rep = lambda t: jnp.repeat(t, h, axis=1)
    tg = _tile(g, 64)
    spec = pl.BlockSpec((tg, w), lambda i: (i, 0))
    npw = SUBLANES + 1
    spec3 = pl.BlockSpec((npw, tg, w), lambda i: (0, i, 0))
    log2_sub = sub_rows.bit_length() - 1
    assert 1 << log2_sub == sub_rows
    bb_re, bb_im, pw_re, pw_im = pl.pallas_call(
        functools.partial(_s5_prep_kernel, log2_sub=log2_sub),
        grid=(g // tg,),
        in_specs=[spec] * 5,
        out_specs=[spec, spec, spec3, spec3],
        out_shape=[jax.ShapeDtypeStruct((g, w), F32)] * 2 + [jax.ShapeDtypeStruct((npw, g, w), F32)] * 2,
        compiler_params=_params("arbitrary"),
        name="s5_prep",
    )(rep(a_re), rep(a_im), jnp.broadcast_to(log_dt[:, None], (g, w)),
      b_re.reshape(g, w), b_im.reshape(g, w))

    nb = g // GROUPS_PER_BLOCK
    eye = jnp.eye(GROUPS_PER_BLOCK, dtype=F32)

    def in_mat(bb):
        t = bb.reshape(nb, GROUPS_PER_BLOCK, p, h).transpose(0, 1, 3, 2)
        t = t[:, :, :, None, :] * eye[None, :, None, :, None]
        return t.reshape(nb, CH_BLOCK, ST_BLOCK)

    def out_mat(c):
        t = c.reshape(nb, GROUPS_PER_BLOCK, h, p).transpose(0, 1, 3, 2)
        t = t[:, :, :, None, :] * eye[None, :, None, :, None]
        return t.reshape(nb, ST_BLOCK, CH_BLOCK)

    b_mat = jnp.concatenate([in_mat(bb_re), in_mat(bb_im)], axis=2).astype(BF16)
    c_mat = jnp.concatenate([out_mat(c_re), -out_mat(c_im)], axis=1).astype(BF16)

    def powers(pw):
        return pw[:, :, ::h].reshape(npw, nb, ST_BLOCK).transpose(1, 0, 2)

    pr, pi = powers(pw_re), powers(pw_im)
    ones = jnp.ones((1, SUBLANES, 1), F32)
    t_idx = jnp.arange(SUBLANES)[None, :, None]
    tabs = [pr[:, 0:1, :] * ones, pi[:, 0:1, :] * ones]
    for dshift in (1, 2, 4):
        keep = (t_idx >= dshift).astype(F32)
        tabs += [pr[:, dshift:dshift + 1, :] * keep, pi[:, dshift:dshift + 1, :] * keep]
    tabs += [pr[:, 1:, :], pi[:, 1:, :]]
    return b_mat, c_mat, jnp.stack(tabs, axis=1)


def _s5_kernel(u_ref, pmt_ref, b_ref, c_ref, t_ref, d_ref, o_ref, st_ref, carry_ref, *, tt):
    ci = pl.program_id(1)
    nlb = ST_BLOCK // LANES
    sub_rows = tt // SUBLANES

    @pl.when(ci == 0)
    def _():
        carry_ref[...] = jnp.zeros(carry_ref.shape, F32)

    def lanes(j):
        return (slice(j * LANES, (j + 1) * LANES),
                slice(ST_BLOCK + j * LANES, ST_BLOCK + (j + 1) * LANES))

    def project_in(h):
        u = u_ref[:, h * CH_BLOCK:(h + 1) * CH_BLOCK]
        st_ref[h] = jnp.dot(u, b_ref[h], preferred_element_type=F32)
        return u

    def scan(h, init, store):
        def step(r, x):
            row = slice(r * SUBLANES, (r + 1) * SUBLANES)
            out = []
            for j in range(nlb):
                lo, hi = lanes(j)
                ar, ai = t_ref[h, 0, :, lo], t_ref[h, 1, :, lo]
                xr, xi = x[2 * j], x[2 * j + 1]
                nr = ar * xr - ai * xi + st_ref[h, row, lo]
                ni = ar * xi + ai * xr + st_ref[h, row, hi]
                if store:
                    st_ref[h, row, lo] = nr
                    st_ref[h, row, hi] = ni
                out += [nr, ni]
            return tuple(out)

        x = init
        for r in range(sub_rows):
            x = step(r, x)
        return x

    def start_states(h, ends):
        first = lax.broadcasted_iota(jnp.int32, (SUBLANES, LANES), 0) == 0
        top = SUBLANES - 1
        init = []
        for j in range(nlb):
            lo, _ = lanes(j)
            yr, yi = ends[2 * j], ends[2 * j + 1]
            for n, dshift in enumerate((1, 2, 4)):
                ar, ai = t_ref[h, 2 + 2 * n, :, lo], t_ref[h, 3 + 2 * n, :, lo]
                rr = pltpu.roll(yr, dshift, 0)
                ri = pltpu.roll(yi, dshift, 0)
                yr, yi = yr + ar * rr - ai * ri, yi + ar * ri + ai * rr
            pr, pi = t_ref[h, 8, :, lo], t_ref[h, 9, :, lo]
            cr, cim = carry_ref[h, 0, :, lo], carry_ref[h, 1, :, lo]
            er = yr + pr * cr - pi * cim
            ei = yi + pr * cim + pi * cr
            init += [jnp.where(first, cr, pltpu.roll(er, 1, 0)),
                     jnp.where(first, cim, pltpu.roll(ei, 1, 0))]
            carry_ref[h, 0, :, lo] = jnp.broadcast_to(er[top:top + 1, :], er.shape)
            carry_ref[h, 1, :, lo] = jnp.broadcast_to(ei[top:top + 1, :], ei.shape)
        return tuple(init)

    def project_out(h, u):
        ch = slice(h * CH_BLOCK, (h + 1) * CH_BLOCK)
        y = jnp.dot(st_ref[h].astype(BF16), c_ref[h], preferred_element_type=F32)
        y = y + d_ref[:, ch] * u.astype(F32)
        g = jax.nn.gelu(y).astype(BF16)
        o_ref[:, ch] = jnp.dot(pmt_ref[...], g, preferred_element_type=F32).astype(o_ref.dtype)

    zero = (jnp.zeros((SUBLANES, LANES), F32),) * (2 * nlb)
    us = [project_in(h) for h in range(BLOCKS_PER_STEP)]
    for h in range(BLOCKS_PER_STEP):
        ends = scan(h, zero, False)
        scan(h, start_states(h, ends), True)
        project_out(h, us[h])


def _s5_tile(l):
    return _tile(l, 512)


def _s5_row_perm(tt):
    rho = jnp.arange(tt)
    pos = (rho % SUBLANES) * (tt // SUBLANES) + rho // SUBLANES
    return (pos[:, None] == jnp.arange(tt)[None, :]).astype(BF16)


def _s5(u, b_mat, c_mat, tabs, d_skip):
    l, d = u.shape
    nb = d // CH_BLOCK
    bps = BLOCKS_PER_STEP
    assert nb % bps == 0
    tt = _s5_tile(l)
    return pl.pallas_call(
        functools.partial(_s5_kernel, tt=tt),
        grid=(nb // bps, l // tt),
        in_specs=[pl.BlockSpec((tt, bps * CH_BLOCK), lambda b, i: (i, b)),
                  pl.BlockSpec((tt, tt), lambda b, i: (0, 0)),
                  pl.BlockSpec((bps, CH_BLOCK, 2 * ST_BLOCK), lambda b, i: (b, 0, 0)),
                  pl.BlockSpec((bps, 2 * ST_BLOCK, CH_BLOCK), lambda b, i: (b, 0, 0)),
                  pl.BlockSpec((bps, 10, SUBLANES, ST_BLOCK), lambda b, i: (b, 0, 0, 0)),
                  pl.BlockSpec((1, bps * CH_BLOCK), lambda b, i: (0, b))],
        out_specs=pl.BlockSpec((tt, bps * CH_BLOCK), lambda b, i: (i, b)),
        out_shape=jax.ShapeDtypeStruct((l, d), BF16),
        scratch_shapes=[pltpu.VMEM((bps, tt, 2 * ST_BLOCK), F32),
                        pltpu.VMEM((bps, 2, SUBLANES, ST_BLOCK), F32)],
        compiler_params=_params("arbitrary", "arbitrary"),
        name="s5_scan",
    )(u, _s5_row_perm(tt).T, b_mat, c_mat, tabs, d_skip.reshape(1, d))


def kernel(x, c, w_ada, b_ada, ada_table, norm_mix, norm_ffn, norm_final, mix_w_in, conv_w, lambda_q1, lambda_k1, lambda_q2, lambda_k2, subln_g, mix_w_out, s5_a_re, s5_a_im, s5_log_dt, s5_b_re, s5_b_im, s5_c_re, s5_c_im, s5_d, glu_w1, glu_w2, ffn_w_gate, ffn_w_up, ffn_w_down):
    bsz, seq, d = x.shape
    assert bsz == 1
    depth = ada_table.shape[0]
    conv_ch = conv_w.shape[2]
    attn_width = mix_w_out.shape[1] - conv_ch
    hidden = ffn_w_gate.shape[2]
    hidden_pad = _round_up(hidden, 512)
    tk_down = hidden_pad // 4
    assert tk_down % LANES == 0 and hidden_pad - hidden < min(512, tk_down)

    w_in = mix_w_in.astype(BF16)
    w_out = mix_w_out.astype(BF16)
    w_glu1 = glu_w1.astype(BF16)
    w_glu2 = glu_w2.astype(BF16)
    w_gate = ffn_w_gate.astype(BF16)
    w_up = ffn_w_up.astype(BF16)
    w_down = ffn_w_down.astype(BF16)

    mods = _ada(c, w_ada, b_ada, ada_table)
    xs = x.reshape(seq, d)
    for l in range(depth):
        r = l * N_MOD
        s5_layer = l % 2 == 1
        h = _norm_mod(xs, norm_mix[l], mods, r + 0, r + 1,
                      row_perm=_s5_row_perm(_s5_tile(seq)) if s5_layer else None)
        if not s5_layer:
            e = l // 2
            lam_init = 0.8 - 0.6 * math.exp(-0.3 * l)
            z = _matmul(h, w_in, e)
            y_conv = _conv(z, conv_w[e])
            lam_vecs = jnp.stack([lambda_q1[e], lambda_k1[e], lambda_q2[e], lambda_k2[e]])
            y_attn = _attention(z, lam_vecs, subln_g[e], lam_init, conv_ch, attn_width)
            xs = _mm2_res(y_conv, y_attn, w_out, e, xs, mods, r + 2)
        else:
            o = l // 2
            b_mat, c_mat, tabs = _s5_prepare(s5_a_re[o], s5_a_im[o], s5_log_dt[o],
                                             s5_b_re[o], s5_b_im[o], s5_c_re[o], s5_c_im[o],
                                             _s5_tile(seq) // SUBLANES)
            g = _s5(h, b_mat, c_mat, tabs, s5_d[o])
            xs = _glu_res(g, w_glu1, w_glu2, o, xs, mods, r + 2)
        h = _norm_mod(xs, norm_ffn[l], mods, r + 3, r + 4)
        act = _swiglu_up(h, w_gate, w_up, l, hidden_pad)
        xs = _mm_res(act, w_down, l, xs, mods, r + 5, tk_down)
    return _final_norm(xs, norm_final).reshape(bsz, seq, d)
```

```python
import functools
import math

import jax
import jax.numpy as jnp
from jax import lax
from jax.experimental import pallas as pl
from jax.experimental.pallas import tpu as pltpu

F32 = jnp.float32
BF16 = jnp.bfloat16

N_MOD = 6
CONV_K = 3
ATTN_HEAD_DIM = 128
S5_GROUP = 16
S5_STATE = 64
NORM_EPS = 1e-6
SUBLN_EPS = 1e-5

GROUPS_PER_BLOCK = 16
CH_BLOCK = GROUPS_PER_BLOCK * S5_GROUP
ST_BLOCK = GROUPS_PER_BLOCK * S5_STATE
BLOCKS_PER_STEP = 2
SUBLANES = 8
LANES = 128
MASK_VALUE = -1e30
MIN_NORMALISER = 2.0 ** -80
VMEM_LIMIT = 56 * 1024 * 1024


def _params(*sem):
    return pltpu.CompilerParams(dimension_semantics=sem, vmem_limit_bytes=VMEM_LIMIT)


def _tile(dim, pref):
    t = min(pref, dim)
    while dim % t:
        t //= 2
    return t


def _round_up(n, m):
    return -(-n // m) * m


def _ada_kernel(c_ref, w_ref, b_ref, t_ref, o_ref):
    c = c_ref[...]
    a = (c * jax.nn.sigmoid(c)).astype(BF16)
    r = jnp.dot(a, w_ref[...].astype(BF16), preferred_element_type=F32)
    o_ref[...] = r[0:1, :] + b_ref[...] + t_ref[...]


def _ada(c, w_ada, b_ada, ada_table):
    d = c.shape[1]
    depth = ada_table.shape[0]
    n = w_ada.shape[1]
    tn = _tile(n, 512)
    c8 = jnp.broadcast_to(c, (SUBLANES, d))
    out = pl.pallas_call(
        _ada_kernel,
        grid=(n // tn,),
        in_specs=[pl.BlockSpec((SUBLANES, d), lambda j: (0, 0)),
                  pl.BlockSpec((d, tn), lambda j: (0, j)),
                  pl.BlockSpec((1, tn), lambda j: (0, j)),
                  pl.BlockSpec((depth, tn), lambda j: (0, j))],
        out_specs=pl.BlockSpec((depth, tn), lambda j: (0, j)),
        out_shape=jax.ShapeDtypeStruct((depth, n), F32),
        compiler_params=_params("arbitrary"),
        name="ada_proj",
    )(c8, w_ada, b_ada.reshape(1, n), ada_table.reshape(depth, n))
    return out.reshape(depth * N_MOD, d)


def _norm_mod_kernel(x_ref, g_ref, m_ref, *rest, shift_row, scale_row):
    o_ref = rest[-1]
    x = x_ref[...]
    ms = jnp.mean(x * x, axis=-1, keepdims=True)
    y = x * lax.rsqrt(ms + NORM_EPS) * g_ref[...]
    shift = m_ref[shift_row:shift_row + 1, :]
    scale = m_ref[scale_row:scale_row + 1, :]
    h = (y * (1.0 + scale) + shift).astype(o_ref.dtype)
    if len(rest) == 2:
        h = jnp.dot(rest[0][...], h, preferred_element_type=F32).astype(o_ref.dtype)
    o_ref[...] = h


def _norm_mod(x, g, mods, shift_row, scale_row, row_perm=None):
    l, d = x.shape
    tm = _tile(l, 512)
    extra_specs, extra = [], []
    if row_perm is not None:
        assert row_perm.shape == (tm, tm)
        extra_specs, extra = [pl.BlockSpec((tm, tm), lambda i: (0, 0))], [row_perm]
    return pl.pallas_call(
        functools.partial(_norm_mod_kernel, shift_row=shift_row, scale_row=scale_row),
        grid=(l // tm,),
        in_specs=[pl.BlockSpec((tm, d), lambda i: (i, 0)),
                  pl.BlockSpec((1, d), lambda i: (0, 0)),
                  pl.BlockSpec(mods.shape, lambda i: (0, 0))] + extra_specs,
        out_specs=pl.BlockSpec((tm, d), lambda i: (i, 0)),
        out_shape=jax.ShapeDtypeStruct((l, d), BF16),
        compiler_params=_params("arbitrary"),
        name="norm_mod",
    )(x, g.reshape(1, d), mods, *extra)


def _norm_kernel(x_ref, g_ref, o_ref):
    x = x_ref[...]
    ms = jnp.mean(x * x, axis=-1, keepdims=True)
    o_ref[...] = x * lax.rsqrt(ms + NORM_EPS) * g_ref[...]


def _final_norm(x, g):
    l, d = x.shape
    tm = _tile(l, 512)
    return pl.pallas_call(
        _norm_kernel,
        grid=(l // tm,),
        in_specs=[pl.BlockSpec((tm, d), lambda i: (i, 0)),
                  pl.BlockSpec((1, d), lambda i: (0, 0))],
        out_specs=pl.BlockSpec((tm, d), lambda i: (i, 0)),
        out_shape=jax.ShapeDtypeStruct((l, d), F32),
        compiler_params=_params("arbitrary"),
        name="final_norm",
    )(x, g.reshape(1, d))


def _mm_kernel(a_ref, w_ref, o_ref):
    o_ref[...] = jnp.dot(a_ref[...], w_ref[...], preferred_element_type=F32).astype(o_ref.dtype)


def _matmul(a, w, layer):
    m, k = a.shape
    n = w.shape[2]
    tm, tn = _tile(m, 1024), _tile(n, 1024)
    return pl.pallas_call(
        _mm_kernel,
        grid=(m // tm, n // tn),
        in_specs=[pl.BlockSpec((tm, k), lambda i, j: (i, 0)),
                  pl.BlockSpec((None, k, tn), lambda i, j: (layer, 0, j))],
        out_specs=pl.BlockSpec((tm, tn), lambda i, j: (i, j)),
        out_shape=jax.ShapeDtypeStruct((m, n), BF16),
        compiler_params=_params("arbitrary", "arbitrary"),
        name="in_proj",
    )(a, w)


def _swiglu_up_kernel(a_ref, wg_ref, wu_ref, *rest, hidden, n_cast):
    cast_in, o_ref, cast_out = rest[:n_cast], rest[n_cast], rest[n_cast + 1:]
    a = a_ref[...]
    g = jnp.dot(a, wg_ref[...], preferred_element_type=F32)
    u = jnp.dot(a, wu_ref[...], preferred_element_type=F32)
    act = g * jax.nn.sigmoid(g) * u
    tn = act.shape[1]
    col = pl.program_id(1) * tn + lax.broadcasted_iota(jnp.int32, (1, tn), 1)
    o_ref[...] = jnp.where(col < hidden, act, 0.0).astype(o_ref.dtype)
    for src, dst in zip(cast_in, cast_out):
        dst[...] = src[...].astype(dst.dtype)


def _cast_rows(rows, steps):
    return min(r for r in range(16, rows + 1, 16) if rows % r == 0 and rows // r <= steps)


def _swiglu_up(a, wg, wu, layer, n_pad, cast=(), cast_layer=0):
    m, k = a.shape
    hidden = wg.shape[2]
    tm, tn = _tile(m, 1024), _tile(n_pad, 512)
    nj = n_pad // tn
    steps = (m // tm) * nj
    cast_specs, cast_shapes = [], []
    for w in cast:
        rows, cols = w.shape[1:]
        rb = _cast_rows(rows, steps)
        last = rows // rb - 1
        cast_specs.append((pl.BlockSpec((None, rb, cols), lambda i, j, last=last: (cast_layer, jnp.minimum(i * nj + j, last), 0)),
                           pl.BlockSpec((None, rb, cols), lambda i, j, last=last: (0, jnp.minimum(i * nj + j, last), 0))))
        cast_shapes.append(jax.ShapeDtypeStruct((1, rows, cols), BF16))
    outs = pl.pallas_call(
        functools.partial(_swiglu_up_kernel, hidden=hidden, n_cast=len(cast)),
        grid=(m // tm, nj),
        in_specs=[pl.BlockSpec((tm, k), lambda i, j: (i, 0)),
                  pl.BlockSpec((None, k, tn), lambda i, j: (layer, 0, j)),
                  pl.BlockSpec((None, k, tn), lambda i, j: (layer, 0, j))] + [s[0] for s in cast_specs],
        out_specs=[pl.BlockSpec((tm, tn), lambda i, j: (i, j))] + [s[1] for s in cast_specs],
        out_shape=[jax.ShapeDtypeStruct((m, n_pad), BF16)] + cast_shapes,
        compiler_params=_params("arbitrary", "arbitrary"),
        name="ffn_up",
    )(a, wg, wu, *cast)
    return outs


def _glu_res_kernel(a_ref, w1_ref, w2_ref, x_ref, m_ref, o_ref, *, gate_row):
    a = a_ref[...]
    y1 = jnp.dot(a, w1_ref[...], preferred_element_type=F32)
    y2 = jnp.dot(a, w2_ref[...], preferred_element_type=F32)
    gate = m_ref[gate_row:gate_row + 1, :]
    o_ref[...] = x_ref[...] + gate * (y1 * jax.nn.sigmoid(y2))


def _glu_res(a, w1, w2, layer, x, mods, gate_row):
    m, k = a.shape
    n = w1.shape[2]
    tm, tn = _tile(m, 1024), _tile(n, 512)
    return pl.pallas_call(
        functools.partial(_glu_res_kernel, gate_row=gate_row),
        grid=(m // tm, n // tn),
        in_specs=[pl.BlockSpec((tm, k), lambda i, j: (i, 0)),
                  pl.BlockSpec((None, k, tn), lambda i, j: (layer, 0, j)),
                  pl.BlockSpec((None, k, tn), lambda i, j: (layer, 0, j)),
                  pl.BlockSpec((tm, tn), lambda i, j: (i, j)),
                  pl.BlockSpec((mods.shape[0], tn), lambda i, j: (0, j))],
        out_specs=pl.BlockSpec((tm, tn), lambda i, j: (i, j)),
        out_shape=jax.ShapeDtypeStruct((m, n), F32),
        compiler_params=_params("arbitrary", "arbitrary"),
        name="glu_res",
    )(a, w1, w2, x, mods)


def _mm_res_kernel(a_ref, w_ref, x_ref, m_ref, o_ref, acc_ref, *, gate_row, nk, k_valid):
    kk = pl.program_id(2)
    w = w_ref[...]
    tk = w.shape[0]
    row = kk * tk + lax.broadcasted_iota(jnp.int32, (tk, 1), 0)
    w = jnp.where(row < k_valid, w, jnp.zeros_like(w))
    part = jnp.dot(a_ref[...], w, preferred_element_type=F32)

    @pl.when(kk == 0)
    def _():
        acc_ref[...] = part

    @pl.when(kk > 0)
    def _():
        acc_ref[...] += part

    @pl.when(kk == nk - 1)
    def _():
        gate = m_ref[gate_row:gate_row + 1, :]
        o_ref[...] = x_ref[...] + gate * acc_ref[...]


def _mm_res(a, w, layer, x, mods, gate_row, tk):
    m, k = a.shape
    n = w.shape[2]
    tm, tn = _tile(m, 1024), _tile(n, 1024)
    nk = k // tk
    return pl.pallas_call(
        functools.partial(_mm_res_kernel, gate_row=gate_row, nk=nk, k_valid=w.shape[1]),
        grid=(m // tm, n // tn, nk),
        in_specs=[pl.BlockSpec((tm, tk), lambda i, j, kk: (i, kk)),
                  pl.BlockSpec((None, tk, tn), lambda i, j, kk: (layer, kk, j)),
                  pl.BlockSpec((tm, tn), lambda i, j, kk: (i, j)),
                  pl.BlockSpec((mods.shape[0], tn), lambda i, j, kk: (0, j))],
        out_specs=pl.BlockSpec((tm, tn), lambda i, j, kk: (i, j)),
        out_shape=jax.ShapeDtypeStruct((m, n), F32),
        scratch_shapes=[pltpu.VMEM((tm, tn), F32)],
        compiler_params=_params("arbitrary", "arbitrary", "arbitrary"),
        name="ffn_down",
    )(a, w, x, mods)


def _mm2_res_kernel(a1_ref, a2_ref, w1_ref, w2_ref, x_ref, m_ref, o_ref, *, gate_row):
    y = jnp.dot(a1_ref[...], w1_ref[...], preferred_element_type=F32)
    y = y + jnp.dot(a2_ref[...], w2_ref[...], preferred_element_type=F32)
    gate = m_ref[gate_row:gate_row + 1, :]
    o_ref[...] = x_ref[...] + gate * y


def _mm2_res(a1, a2, w, layer, x, mods, gate_row):
    m, k1 = a1.shape
    k2 = a2.shape[1]
    assert k1 == k2 and w.shape[1] == k1 + k2
    n = w.shape[2]
    tm, tn = _tile(m, 1024), _tile(n, 512)
    return pl.pallas_call(
        functools.partial(_mm2_res_kernel, gate_row=gate_row),
        grid=(m // tm, n // tn),
        in_specs=[pl.BlockSpec((tm, k1), lambda i, j: (i, 0)),
                  pl.BlockSpec((tm, k2), lambda i, j: (i, 0)),
                  pl.BlockSpec((None, k1, tn), lambda i, j: (layer, 0, j)),
                  pl.BlockSpec((None, k2, tn), lambda i, j: (layer, 1, j)),
                  pl.BlockSpec((tm, tn), lambda i, j: (i, j)),
                  pl.BlockSpec((mods.shape[0], tn), lambda i, j: (0, j))],
        out_specs=pl.BlockSpec((tm, tn), lambda i, j: (i, j)),
        out_shape=jax.ShapeDtypeStruct((m, n), F32),
        compiler_params=_params("arbitrary", "arbitrary"),
        name="out_proj",
    )(a1, a2, w, w, x, mods)


def _conv_kernel(b_ref, c_ref, xi_ref, ch_ref, xh_ref, w_ref, o_ref):
    i = pl.program_id(0)
    p = c_ref[...].astype(F32) * xi_ref[...].astype(F32)
    halo = ch_ref[...].astype(F32) * xh_ref[...].astype(F32)
    halo = halo * jnp.where(i > 0, 1.0, 0.0)
    nh = halo.shape[0]
    prev1 = halo[nh - 1:nh, :]
    prev2 = halo[nh - 2:nh - 1, :]
    row = lax.broadcasted_iota(jnp.int32, p.shape, 0)
    p1 = jnp.where(row == 0, prev1, pltpu.roll(p, 1, 0))
    p2 = jnp.where(row == 0, prev2, jnp.where(row == 1, prev1, pltpu.roll(p, 2, 0)))
    w = w_ref[...]
    y = w[2:3, :] * p + w[1:2, :] * p1 + w[0:1, :] * p2
    o_ref[...] = (b_ref[...].astype(F32) * y).astype(o_ref.dtype)


def _conv(z, conv_w):
    l = z.shape[0]
    ch = conv_w.shape[1]
    tm, tc = _tile(l, 512), _tile(ch, 512)
    halo = 16
    nc = ch // tc
    rb = tm // halo
    return pl.pallas_call(
        _conv_kernel,
        grid=(l // tm, nc),
        in_specs=[pl.BlockSpec((tm, tc), lambda i, j: (i, j)),
                  pl.BlockSpec((tm, tc), lambda i, j: (i, nc + j)),
                  pl.BlockSpec((tm, tc), lambda i, j: (i, 2 * nc + j)),
                  pl.BlockSpec((halo, tc), lambda i, j: (jnp.maximum(i * rb - 1, 0), nc + j)),
                  pl.BlockSpec((halo, tc), lambda i, j: (jnp.maximum(i * rb - 1, 0), 2 * nc + j)),
                  pl.BlockSpec((CONV_K, tc), lambda i, j: (0, j))],
        out_specs=pl.BlockSpec((tm, tc), lambda i, j: (i, j)),
        out_shape=jax.ShapeDtypeStruct((l, ch), BF16),
        compiler_params=_params("arbitrary", "arbitrary"),
        name="gated_conv",
    )(z, z, z, z, z, conv_w)


def _attn_kernel(lam_ref, g_ref, q_ref, k_ref, v_ref, o_ref,
                 m1_ref, l1_ref, a1_ref, m2_ref, l2_ref, a2_ref,
                 sa_ref, sb_ref, pa_ref, pb_ref, ala_ref, alb_ref, kn_ref, *, tq, tk, lam_init):
    qi = pl.program_id(1)
    d = ATTN_HEAD_DIM
    q = q_ref[...].astype(F32) * (d ** -0.5 * math.log2(math.e))
    q1 = q[:, :d].astype(BF16)
    q2 = q[:, d:].astype(BF16)

    def reset():
        m1_ref[...] = jnp.full(m1_ref.shape, MASK_VALUE, F32)
        m2_ref[...] = jnp.full(m2_ref.shape, MASK_VALUE, F32)
        l1_ref[...] = jnp.zeros(l1_ref.shape, F32)
        l2_ref[...] = jnp.zeros(l2_ref.shape, F32)
        a1_ref[...] = jnp.zeros(a1_ref.shape, F32)
        a2_ref[...] = jnp.zeros(a2_ref.shape, F32)

    nt = (((1,), (1,)), ((), ()))
    nch = tk // LANES
    rc = min(tq, 64)
    last = (qi * tq) // tk
    diag_off = qi * tq - last * tk

    def softmax_map(bufs, mi, masked, m_ref, l_ref):
        s_ref, p_ref, al_ref = bufs
        for r0 in range(0, tq, rc):
            rows = slice(r0, r0 + rc)
            s = s_ref[mi, rows, :]
            if masked:
                keep = (lax.broadcasted_iota(jnp.int32, s.shape, 1)
                        <= lax.broadcasted_iota(jnp.int32, s.shape, 0) + (r0 + diag_off))
                s = jnp.where(keep, s, MASK_VALUE)
            cols = [s[:, i * LANES:(i + 1) * LANES] for i in range(nch)]
            mx = functools.reduce(jnp.maximum, cols)
            m_old = m_ref[rows, :]
            m_new = jnp.maximum(m_old, jnp.max(mx, axis=1, keepdims=True))
            alpha = jnp.exp2(m_old - m_new)
            ps = [jnp.exp2(c - m_new) for c in cols]
            l_ref[rows, :] = alpha * l_ref[rows, :] + functools.reduce(jnp.add, ps)
            p_ref[mi, rows, :] = jnp.concatenate([x.astype(BF16) for x in ps], axis=1)
            al_ref[mi, rows, :] = alpha
            m_ref[rows, :] = m_new

    def pv_map(bufs, mi, v, a_ref):
        _, p_ref, al_ref = bufs
        pv = jnp.dot(p_ref[mi], v, preferred_element_type=F32)
        alpha = al_ref[mi]
        a_ref[...] = jnp.concatenate([alpha, alpha], axis=1) * a_ref[...] + pv

    def scores(kb, bufs):
        s_ref = bufs[0]
        ks = pl.multiple_of(kb * tk, tk)
        k = k_ref[pl.ds(ks, tk), :]
        s_ref[0] = lax.dot_general(q1, k[:, :d], nt, preferred_element_type=F32)
        s_ref[1] = lax.dot_general(q2, k[:, d:], nt, preferred_element_type=F32)

    def consume(kb, bufs, masked):
        ks = pl.multiple_of(kb * tk, tk)
        v = v_ref[pl.ds(ks, tk), :]
        softmax_map(bufs, 0, masked, m1_ref, l1_ref)
        pv_map(bufs, 0, v, a1_ref)
        softmax_map(bufs, 1, masked, m2_ref, l2_ref)
        pv_map(bufs, 1, v, a2_ref)

    def exact_path():
        reset()
        buf_a = (sa_ref, pa_ref, ala_ref)
        buf_b = (sb_ref, pb_ref, alb_ref)
        scores(0, buf_a)

        def body(j, carry):
            scores(2 * j + 1, buf_b)
            consume(2 * j, buf_a, False)
            scores(2 * j + 2, buf_a)
            consume(2 * j + 1, buf_b, False)
            return carry

        lax.fori_loop(0, last // 2, body, 0)

        @pl.when(last % 2 == 1)
        def _():
            scores(last, buf_b)
            consume(last - 1, buf_a, False)
            consume(last, buf_b, True)

        @pl.when(last % 2 == 0)
        def _():
            consume(last, buf_a, True)

    @pl.when(qi == 0)
    def _():
        def kbody(t, c):
            kt = k_ref[pl.ds(pl.multiple_of(t * tk, tk), tk), :].astype(F32)
            k1, k2 = kt[:, :d], kt[:, d:]
            n1 = jnp.max(jnp.sum(k1 * k1, axis=1, keepdims=True), axis=0, keepdims=True)
            n2 = jnp.max(jnp.sum(k2 * k2, axis=1, keepdims=True), axis=0, keepdims=True)
            return jnp.maximum(c[0], n1), jnp.maximum(c[1], n2)

        zero11 = jnp.zeros((1, 1), F32)
        n1, n2 = lax.fori_loop(0, k_ref.shape[0] // tk, kbody, (zero11, zero11))
        kn_ref[0] = jnp.broadcast_to(jnp.sqrt(n1), (SUBLANES, LANES))
        kn_ref[1] = jnp.broadcast_to(jnp.sqrt(n2), (SUBLANES, LANES))

    def row_bound(qh, mi):
        qf = qh.astype(F32)
        qn = jnp.sqrt(jnp.sum(qf * qf, axis=1, keepdims=True))
        return jnp.broadcast_to(qn * kn_ref[mi, 0:1, 0:1], (tq, LANES))

    b1 = row_bound(q1, 0)
    b2 = row_bound(q2, 1)

    def fast_map(s, v, bound, masked, l_ref, a_ref):
        if masked:
            keep = (lax.broadcasted_iota(jnp.int32, s.shape, 1)
                    <= lax.broadcasted_iota(jnp.int32, s.shape, 0))
            s = jnp.where(keep, s, MASK_VALUE)
        ps = [jnp.exp2(s[:, i * LANES:(i + 1) * LANES] - bound) for i in range(s.shape[1] // LANES)]
        l_ref[...] += functools.reduce(jnp.add, ps)
        p = jnp.concatenate([x.astype(BF16) for x in ps], axis=1)
        a_ref[...] += jnp.dot(p, v, preferred_element_type=F32)

    def fast_step(ks, width, masked):
        ks = pl.multiple_of(ks, width)
        k = k_ref[pl.ds(ks, width), :]
        v = v_ref[pl.ds(ks, width), :]
        s1 = lax.dot_general(q1, k[:, :d], nt, preferred_element_type=F32)
        s2 = lax.dot_general(q2, k[:, d:], nt, preferred_element_type=F32)
        fast_map(s1, v, b1, masked, l1_ref, a1_ref)
        fast_map(s2, v, b2, masked, l2_ref, a2_ref)

    reset()

    def fast_body(j, carry):
        fast_step(2 * j * tk, tk, False)
        fast_step((2 * j + 1) * tk, tk, False)
        return carry

    lax.fori_loop(0, last // 2, fast_body, 0)

    @pl.when(last % 2 == 1)
    def _():
        fast_step((last - 1) * tk, tk, False)

    @pl.when(diag_off > 0)
    def _():
        fast_step(last * tk, tq, False)

    fast_step(qi * tq, tq, True)

    def row_ok(l_ref):
        return jnp.sum(l_ref[...], axis=1, keepdims=True) >= MIN_NORMALISER
    healthy = jnp.where(jnp.logical_and(row_ok(l1_ref), row_ok(l2_ref)), 1.0, 0.0)
    all_healthy = jnp.min(healthy, axis=0, keepdims=True)[0, 0] > 0.5

    @pl.when(jnp.logical_not(all_healthy))
    def _():
        exact_path()

    lv = lam_ref[...]
    lam = (jnp.exp(jnp.sum(lv[0:1, :] * lv[1:2, :], axis=-1, keepdims=True))
           - jnp.exp(jnp.sum(lv[2:3, :] * lv[3:4, :], axis=-1, keepdims=True)) + lam_init)
    l1 = jnp.sum(l1_ref[...], axis=1, keepdims=True)
    l2 = jnp.sum(l2_ref[...], axis=1, keepdims=True)
    o = a1_ref[...] / l1 - lam * (a2_ref[...] / l2)
    ms = jnp.mean(o * o, axis=-1, keepdims=True)
    o = o * lax.rsqrt(ms + SUBLN_EPS) * g_ref[...] * (1.0 - lam_init)
    o_ref[...] = o.astype(o_ref.dtype)


def _attention(z, lam_vecs, subln_g, lam_init, conv_ch, attn_width):
    l = z.shape[0]
    hw = 2 * ATTN_HEAD_DIM
    heads = attn_width // hw
    tq, tk = _tile(l, 512), _tile(l, 1024)
    assert tk in (tq, 2 * tq)
    qoff = 3 * conv_ch // hw
    koff = qoff + heads
    voff = koff + heads
    once = pl.Buffered(1)
    return pl.pallas_call(
        functools.partial(_attn_kernel, tq=tq, tk=tk, lam_init=lam_init),
        grid=(heads, l // tq),
        in_specs=[pl.BlockSpec((4, ATTN_HEAD_DIM), lambda h, i: (0, 0)),
                  pl.BlockSpec((1, hw), lambda h, i: (0, 0)),
                  pl.BlockSpec((tq, hw), lambda h, i: (i, qoff + h)),
                  pl.BlockSpec((l, hw), lambda h, i: (0, koff + h), pipeline_mode=once),
                  pl.BlockSpec((l, hw), lambda h, i: (0, voff + h), pipeline_mode=once)],
        out_specs=pl.BlockSpec((tq, hw), lambda h, i: (i, h)),
        out_shape=jax.ShapeDtypeStruct((l, attn_width), BF16),
        scratch_shapes=[pltpu.VMEM((tq, LANES), F32), pltpu.VMEM((tq, LANES), F32), pltpu.VMEM((tq, hw), F32),
                        pltpu.VMEM((tq, LANES), F32), pltpu.VMEM((tq, LANES), F32), pltpu.VMEM((tq, hw), F32),
                        pltpu.VMEM((2, tq, tk), F32), pltpu.VMEM((2, tq, tk), F32),
                        pltpu.VMEM((2, tq, tk), BF16), pltpu.VMEM((2, tq, tk), BF16),
                        pltpu.VMEM((2, tq, LANES), F32), pltpu.VMEM((2, tq, LANES), F32),
                        pltpu.VMEM((2, SUBLANES, LANES), F32)],
        compiler_params=_params("arbitrary", "arbitrary"),
        name="diff_attn",
    )(lam_vecs, subln_g.reshape(1, hw), z, z, z)


def _s5_prep_kernel(are_ref, aim_ref, ldt_ref, bre_ref, bim_ref, bbre_ref, bbim_ref, pwre_ref, pwim_ref,
                    *, log2_sub):
    lam_re = jnp.minimum(are_ref[...], -1e-4)
    lam_im = aim_ref[...]
    dt = jnp.exp(ldt_ref[...])
    mag = jnp.exp(lam_re * dt)
    ab_re = mag * jnp.cos(lam_im * dt)
    ab_im = mag * jnp.sin(lam_im * dt)
    den = lam_re * lam_re + lam_im * lam_im
    nr, ni = ab_re - 1.0, ab_im
    f_re = (nr * lam_re + ni * lam_im) / den
    f_im = (ni * lam_re - nr * lam_im) / den
    b_re, b_im = bre_ref[...], bim_ref[...]
    bbre_ref[...] = f_re * b_re - f_im * b_im
    bbim_ref[...] = f_re * b_im + f_im * b_re
    pwre_ref[0] = ab_re
    pwim_ref[0] = ab_im
    sr, si = ab_re, ab_im
    for _ in range(log2_sub):
        sr, si = sr * sr - si * si, 2.0 * sr * si
    pr, pi = sr, si
    for k in range(1, SUBLANES + 1):
        pwre_ref[k] = pr
        pwim_ref[k] = pi
        pr, pi = pr * sr - pi * si, pr * si + pi * sr


def _s5_prepare(a_re, a_im, log_dt, b_re, b_im, c_re, c_im, sub_rows):
    g, p = a_re.shape
    h = S5_GROUP
    w = p * h
    rep = lambda t: jnp.repeat(t, h, axis=1)
    tg = _tile(g, 64)
    spec = pl.BlockSpec((tg, w), lambda i: (i, 0))
    npw = SUBLANES + 1
    spec3 = pl.BlockSpec((npw, tg, w), lambda i: (0, i, 0))
    log2_sub = sub_rows.bit_length() - 1
    assert 1 << log2_sub == sub_rows
    bb_re, bb_im, pw_re, pw_im = pl.pallas_call(
        functools.partial(_s5_prep_kernel, log2_sub=log2_sub),
        grid=(g // tg,),
        in_specs=[spec] * 5,
        out_specs=[spec, spec, spec3, spec3],
        out_shape=[jax.ShapeDtypeStruct((g, w), F32)] * 2 + [jax.ShapeDtypeStruct((npw, g, w), F32)] * 2,
        compiler_params=_params("arbitrary"),
        name="s5_prep",
    )(rep(a_re), rep(a_im), jnp.broadcast_to(log_dt[:, None], (g, w)),
      b_re.reshape(g, w), b_im.reshape(g, w))

    nb = g // GROUPS_PER_BLOCK
    eye = jnp.eye(GROUPS_PER_BLOCK, dtype=F32)

    def in_mat(bb):
        t = bb.reshape(nb, GROUPS_PER_BLOCK, p, h).transpose(0, 1, 3, 2)
        t = t[:, :, :, None, :] * eye[None, :, None, :, None]
        return t.reshape(nb, CH_BLOCK, ST_BLOCK)

    def out_mat(c):
        t = c.reshape(nb, GROUPS_PER_BLOCK, h, p).transpose(0, 1, 3, 2)
        t = t[:, :, :, None, :] * eye[None, :, None, :, None]
        return t.reshape(nb, ST_BLOCK, CH_BLOCK)

    b_mat = jnp.concatenate([in_mat(bb_re), in_mat(bb_im)], axis=2).astype(BF16)
    c_mat = jnp.concatenate([out_mat(c_re), -out_mat(c_im)], axis=1).astype(BF16)

    def powers(pw):
        return pw[:, :, ::h].reshape(npw, nb, ST_BLOCK).transpose(1, 0, 2)

    pr, pi = powers(pw_re), powers(pw_im)
    ones = jnp.ones((1, SUBLANES, 1), F32)
    t_idx = jnp.arange(SUBLANES)[None, :, None]
    tabs = [pr[:, 0:1, :] * ones, pi[:, 0:1, :] * ones]
    for dshift in (1, 2, 4):
        keep = (t_idx >= dshift).astype(F32)
        tabs += [pr[:, dshift:dshift + 1, :] * keep, pi[:, dshift:dshift + 1, :] * keep]
    tabs += [pr[:, 1:, :], pi[:, 1:, :]]
    return b_mat, c_mat, jnp.stack(tabs, axis=1)


def _s5_kernel(u_ref, pmt_ref, b_ref, c_ref, t_ref, d_ref, o_ref, st_ref, carry_ref, *, tt):
    ci = pl.program_id(1)
    nlb = ST_BLOCK // LANES
    sub_rows = tt // SUBLANES

    @pl.when(ci == 0)
    def _():
        carry_ref[...] = jnp.zeros(carry_ref.shape, F32)

    def lanes(j):
        return (slice(j * LANES, (j + 1) * LANES),
                slice(ST_BLOCK + j * LANES, ST_BLOCK + (j + 1) * LANES))

    def project_in(h):
        u = u_ref[:, h * CH_BLOCK:(h + 1) * CH_BLOCK]
        st_ref[h] = jnp.dot(u, b_ref[h], preferred_element_type=F32)
        return u

    def scan(h, init, store):
        def step(r, x):
            row = slice(r * SUBLANES, (r + 1) * SUBLANES)
            out = []
            for j in range(nlb):
                lo, hi = lanes(j)
                ar, ai = t_ref[h, 0, :, lo], t_ref[h, 1, :, lo]
                xr, xi = x[2 * j], x[2 * j + 1]
                nr = ar * xr - ai * xi + st_ref[h, row, lo]
                ni = ar * xi + ai * xr + st_ref[h, row, hi]
                if store:
                    st_ref[h, row, lo] = nr
                    st_ref[h, row, hi] = ni
                out += [nr, ni]
            return tuple(out)

        x = init
        for r in range(sub_rows):
            x = step(r, x)
        return x

    def start_states(h, ends):
        first = lax.broadcasted_iota(jnp.int32, (SUBLANES, LANES), 0) == 0
        top = SUBLANES - 1
        init = []
        for j in range(nlb):
            lo, _ = lanes(j)
            yr, yi = ends[2 * j], ends[2 * j + 1]
            for n, dshift in enumerate((1, 2, 4)):
                ar, ai = t_ref[h, 2 + 2 * n, :, lo], t_ref[h, 3 + 2 * n, :, lo]
                rr = pltpu.roll(yr, dshift, 0)
                ri = pltpu.roll(yi, dshift, 0)
                yr, yi = yr + ar * rr - ai * ri, yi + ar * ri + ai * rr
            pr, pi = t_ref[h, 8, :, lo], t_ref[h, 9, :, lo]
            cr, cim = carry_ref[h, 0, :, lo], carry_ref[h, 1, :, lo]
            er = yr + pr * cr - pi * cim
            ei = yi + pr * cim + pi * cr
            init += [jnp.where(first, cr, pltpu.roll(er, 1, 0)),
                     jnp.where(first, cim, pltpu.roll(ei, 1, 0))]
            carry_ref[h, 0, :, lo] = jnp.broadcast_to(er[top:top + 1, :], er.shape)
            carry_ref[h, 1, :, lo] = jnp.broadcast_to(ei[top:top + 1, :], ei.shape)
        return tuple(init)

    def project_out(h, u):
        ch = slice(h * CH_BLOCK, (h + 1) * CH_BLOCK)
        y = jnp.dot(st_ref[h].astype(BF16), c_ref[h], preferred_element_type=F32)
        y = y + d_ref[:, ch] * u.astype(F32)
        g = jax.nn.gelu(y).astype(BF16)
        o_ref[:, ch] = jnp.dot(pmt_ref[...], g, preferred_element_type=F32).astype(o_ref.dtype)

    zero = (jnp.zeros((SUBLANES, LANES), F32),) * (2 * nlb)
    us = [project_in(h) for h in range(BLOCKS_PER_STEP)]
    for h in range(BLOCKS_PER_STEP):
        ends = scan(h, zero, False)
        scan(h, start_states(h, ends), True)
        project_out(h, us[h])


def _s5_tile(l):
    return _tile(l, 512)


def _s5_row_perm(tt):
    rho = jnp.arange(tt)
    pos = (rho % SUBLANES) * (tt // SUBLANES) + rho // SUBLANES
    return (pos[:, None] == jnp.arange(tt)[None, :]).astype(BF16)


def _s5(u, b_mat, c_mat, tabs, d_skip):
    l, d = u.shape
    nb = d // CH_BLOCK
    bps = BLOCKS_PER_STEP
    assert nb % bps == 0
    tt = _s5_tile(l)
    return pl.pallas_call(
        functools.partial(_s5_kernel, tt=tt),
        grid=(nb // bps, l // tt),
        in_specs=[pl.BlockSpec((tt, bps * CH_BLOCK), lambda b, i: (i, b)),
                  pl.BlockSpec((tt, tt), lambda b, i: (0, 0)),
                  pl.BlockSpec((bps, CH_BLOCK, 2 * ST_BLOCK), lambda b, i: (b, 0, 0)),
                  pl.BlockSpec((bps, 2 * ST_BLOCK, CH_BLOCK), lambda b, i: (b, 0, 0)),
                  pl.BlockSpec((bps, 10, SUBLANES, ST_BLOCK), lambda b, i: (b, 0, 0, 0)),
                  pl.BlockSpec((1, bps * CH_BLOCK), lambda b, i: (0, b))],
        out_specs=pl.BlockSpec((tt, bps * CH_BLOCK), lambda b, i: (i, b)),
        out_shape=jax.ShapeDtypeStruct((l, d), BF16),
        scratch_shapes=[pltpu.VMEM((bps, tt, 2 * ST_BLOCK), F32),
                        pltpu.VMEM((bps, 2, SUBLANES, ST_BLOCK), F32)],
        compiler_params=_params("arbitrary", "arbitrary"),
        name="s5_scan",
    )(u, _s5_row_perm(tt).T, b_mat, c_mat, tabs, d_skip.reshape(1, d))


def kernel(x, c, w_ada, b_ada, ada_table, norm_mix, norm_ffn, norm_final, mix_w_in, conv_w, lambda_q1, lambda_k1, lambda_q2, lambda_k2, subln_g, mix_w_out, s5_a_re, s5_a_im, s5_log_dt, s5_b_re, s5_b_im, s5_c_re, s5_c_im, s5_d, glu_w1, glu_w2, ffn_w_gate, ffn_w_up, ffn_w_down):
    bsz, seq, d = x.shape
    assert bsz == 1
    depth = ada_table.shape[0]
    conv_ch = conv_w.shape[2]
    attn_width = mix_w_out.shape[1] - conv_ch
    hidden = ffn_w_gate.shape[2]
    hidden_pad = _round_up(hidden, 512)
    tk_down = hidden_pad // 4
    assert tk_down % LANES == 0 and hidden_pad - hidden < min(512, tk_down)

    w_in = mix_w_in.astype(BF16)
    w_out = mix_w_out.astype(BF16)
    w_glu1 = glu_w1.astype(BF16)
    w_glu2 = glu_w2.astype(BF16)
    ffn_f32 = (ffn_w_gate, ffn_w_up, ffn_w_down)
    w_gate, w_up, w_down = (w[0:1].astype(BF16) for w in ffn_f32)

    mods = _ada(c, w_ada, b_ada, ada_table)
    xs = x.reshape(seq, d)
    for l in range(depth):
        r = l * N_MOD
        s5_layer = l % 2 == 1
        h = _norm_mod(xs, norm_mix[l], mods, r + 0, r + 1,
                      row_perm=_s5_row_perm(_s5_tile(seq)) if s5_layer else None)
        if not s5_layer:
            e = l // 2
            lam_init = 0.8 - 0.6 * math.exp(-0.3 * l)
            z = _matmul(h, w_in, e)
            y_conv = _conv(z, conv_w[e])
            lam_vecs = jnp.stack([lambda_q1[e], lambda_k1[e], lambda_q2[e], lambda_k2[e]])
            y_attn = _attention(z, lam_vecs, subln_g[e], lam_init, conv_ch, attn_width)
            xs = _mm2_res(y_conv, y_attn, w_out, e, xs, mods, r + 2)
        else:
            o = l // 2
            b_mat, c_mat, tabs = _s5_prepare(s5_a_re[o], s5_a_im[o], s5_log_dt[o],
                                             s5_b_re[o], s5_b_im[o], s5_c_re[o], s5_c_im[o],
                                             _s5_tile(seq) // SUBLANES)
            g = _s5(h, b_mat, c_mat, tabs, s5_d[o])
            xs = _glu_res(g, w_glu1, w_glu2, o, xs, mods, r + 2)
        h = _norm_mod(xs, norm_ffn[l], mods, r + 3, r + 4)
        nxt = ffn_f32 if l + 1 < depth else ()
        act, *w_next = _swiglu_up(h, w_gate, w_up, 0, hidden_pad, cast=nxt, cast_layer=l + 1)
        xs = _mm_res(act, w_down, 0, xs, mods, r + 5, tk_down)
        if nxt:
            w_gate, w_up, w_down = w_next
    return _final_norm(xs, norm_final).reshape(bsz, seq, d)
```

```python
import functools
import math

import jax
import jax.numpy as jnp
from jax import lax
from jax.experimental import pallas as pl
from jax.experimental.pallas import tpu as pltpu

F32 = jnp.float32
BF16 = jnp.bfloat16

N_MOD = 6
CONV_K = 3
ATTN_HEAD_DIM = 128
S5_GROUP = 16
S5_STATE = 64
NORM_EPS = 1e-6
SUBLN_EPS = 1e-5

GROUPS_PER_BLOCK = 16
CH_BLOCK = GROUPS_PER_BLOCK * S5_GROUP
ST_BLOCK = GROUPS_PER_BLOCK * S5_STATE
BLOCKS_PER_STEP = 2
SUBLANES = 8
LANES = 128
MASK_VALUE = -1e30
MIN_NORMALISER = 2.0 ** -80
VMEM_LIMIT = 56 * 1024 * 1024


def _params(*sem):
    return pltpu.CompilerParams(dimension_semantics=sem, vmem_limit_bytes=VMEM_LIMIT)


def _tile(dim, pref):
    t = min(pref, dim)
    while dim % t:
        t //= 2
    return t


def _round_up(n, m):
    return -(-n // m) * m


def _ada_kernel(c_ref, w_ref, b_ref, t_ref, o_ref):
    c = c_ref[...]
    a = (c * jax.nn.sigmoid(c)).astype(BF16)
    r = jnp.dot(a, w_ref[...].astype(BF16), preferred_element_type=F32)
    o_ref[...] = r[0:1, :] + b_ref[...] + t_ref[...]


def _ada(c, w_ada, b_ada, ada_table):
    d = c.shape[1]
    depth = ada_table.shape[0]
    n = w_ada.shape[1]
    tn = _tile(n, 512)
    c8 = jnp.broadcast_to(c, (SUBLANES, d))
    out = pl.pallas_call(
        _ada_kernel,
        grid=(n // tn,),
        in_specs=[pl.BlockSpec((SUBLANES, d), lambda j: (0, 0)),
                  pl.BlockSpec((d, tn), lambda j: (0, j)),
                  pl.BlockSpec((1, tn), lambda j: (0, j)),
                  pl.BlockSpec((depth, tn), lambda j: (0, j))],
        out_specs=pl.BlockSpec((depth, tn), lambda j: (0, j)),
        out_shape=jax.ShapeDtypeStruct((depth, n), F32),
        compiler_params=_params("arbitrary"),
        name="ada_proj",
    )(c8, w_ada, b_ada.reshape(1, n), ada_table.reshape(depth, n))
    return out.reshape(depth * N_MOD, d)


def _norm_mod_kernel(x_ref, g_ref, m_ref, *rest, shift_row, scale_row):
    o_ref = rest[-1]
    x = x_ref[...]
    ms = jnp.mean(x * x, axis=-1, keepdims=True)
    y = x * lax.rsqrt(ms + NORM_EPS) * g_ref[...]
    shift = m_ref[shift_row:shift_row + 1, :]
    scale = m_ref[scale_row:scale_row + 1, :]
    h = (y * (1.0 + scale) + shift).astype(o_ref.dtype)
    if len(rest) == 2:
        h = jnp.dot(rest[0][...], h, preferred_element_type=F32).astype(o_ref.dtype)
    o_ref[...] = h


def _norm_mod(x, g, mods, shift_row, scale_row, row_perm=None):
    l, d = x.shape
    tm = _tile(l, 512)
    extra_specs, extra = [], []
    if row_perm is not None:
        assert row_perm.shape == (tm, tm)
        extra_specs, extra = [pl.BlockSpec((tm, tm), lambda i: (0, 0))], [row_perm]
    return pl.pallas_call(
        functools.partial(_norm_mod_kernel, shift_row=shift_row, scale_row=scale_row),
        grid=(l // tm,),
        in_specs=[pl.BlockSpec((tm, d), lambda i: (i, 0)),
                  pl.BlockSpec((1, d), lambda i: (0, 0)),
                  pl.BlockSpec(mods.shape, lambda i: (0, 0))] + extra_specs,
        out_specs=pl.BlockSpec((tm, d), lambda i: (i, 0)),
        out_shape=jax.ShapeDtypeStruct((l, d), BF16),
        compiler_params=_params("arbitrary"),
        name="norm_mod",
    )(x, g.reshape(1, d), mods, *extra)


def _norm_kernel(x_ref, g_ref, o_ref):
    x = x_ref[...]
    ms = jnp.mean(x * x, axis=-1, keepdims=True)
    o_ref[...] = x * lax.rsqrt(ms + NORM_EPS) * g_ref[...]


def _final_norm(x, g):
    l, d = x.shape
    tm = _tile(l, 512)
    return pl.pallas_call(
        _norm_kernel,
        grid=(l // tm,),
        in_specs=[pl.BlockSpec((tm, d), lambda i: (i, 0)),
                  pl.BlockSpec((1, d), lambda i: (0, 0))],
        out_specs=pl.BlockSpec((tm, d), lambda i: (i, 0)),
        out_shape=jax.ShapeDtypeStruct((l, d), F32),
        compiler_params=_params("arbitrary"),
        name="final_norm",
    )(x, g.reshape(1, d))


def _mm_kernel(a_ref, w_ref, o_ref):
    o_ref[...] = jnp.dot(a_ref[...], w_ref[...], preferred_element_type=F32).astype(o_ref.dtype)


def _matmul(a, w, layer):
    m, k = a.shape
    n = w.shape[2]
    tm, tn = _tile(m, 1024), _tile(n, 1024)
    return pl.pallas_call(
        _mm_kernel,
        grid=(m // tm, n // tn),
        in_specs=[pl.BlockSpec((tm, k), lambda i, j: (i, 0)),
                  pl.BlockSpec((None, k, tn), lambda i, j: (layer, 0, j))],
        out_specs=pl.BlockSpec((tm, tn), lambda i, j: (i, j)),
        out_shape=jax.ShapeDtypeStruct((m, n), BF16),
        compiler_params=_params("arbitrary", "arbitrary"),
        name="in_proj",
    )(a, w)


def _swiglu_up_kernel(a_ref, wg_ref, wu_ref, *rest, hidden, n_cast):
    cast_in, o_ref, cast_out = rest[:n_cast], rest[n_cast], rest[n_cast + 1:]
    a = a_ref[...]
    g = jnp.dot(a, wg_ref[...], preferred_element_type=F32)
    u = jnp.dot(a, wu_ref[...], preferred_element_type=F32)
    act = g * jax.nn.sigmoid(g) * u
    tn = act.shape[1]
    col = pl.program_id(1) * tn + lax.broadcasted_iota(jnp.int32, (1, tn), 1)
    o_ref[...] = jnp.where(col < hidden, act, 0.0).astype(o_ref.dtype)
    for src, dst in zip(cast_in, cast_out):
        dst[...] = src[...].astype(dst.dtype)


def _cast_side_work(cast, cast_layer, grid):
    n_inner = grid[1]
    steps = grid[0] * n_inner
    in_specs, out_specs, shapes = [], [], []
    for w in cast:
        rows, cols = w.shape[1:]
        rb = min(r for r in range(16, rows + 1, 16) if rows % r == 0 and rows // r <= steps)
        last = rows // rb - 1
        in_specs.append(pl.BlockSpec(
            (None, rb, cols), lambda i, j, last=last: (cast_layer, jnp.minimum(i * n_inner + j, last), 0)))
        out_specs.append(pl.BlockSpec(
            (None, rb, cols), lambda i, j, last=last: (0, jnp.minimum(i * n_inner + j, last), 0)))
        shapes.append(jax.ShapeDtypeStruct((1, rows, cols), BF16))
    return in_specs, out_specs, shapes


def _swiglu_up(a, wg, wu, layer, n_pad, cast=(), cast_layer=0):
    m, k = a.shape
    hidden = wg.shape[2]
    tm, tn = _tile(m, 1024), _tile(n_pad, 512)
    grid = (m // tm, n_pad // tn)
    cast_in, cast_out, cast_shapes = _cast_side_work(cast, cast_layer, grid)
    return pl.pallas_call(
        functools.partial(_swiglu_up_kernel, hidden=hidden, n_cast=len(cast)),
        grid=grid,
        in_specs=[pl.BlockSpec((tm, k), lambda i, j: (i, 0)),
                  pl.BlockSpec((None, k, tn), lambda i, j: (layer, 0, j)),
                  pl.BlockSpec((None, k, tn), lambda i, j: (layer, 0, j))] + cast_in,
        out_specs=[pl.BlockSpec((tm, tn), lambda i, j: (i, j))] + cast_out,
        out_shape=[jax.ShapeDtypeStruct((m, n_pad), BF16)] + cast_shapes,
        compiler_params=_params("arbitrary", "arbitrary"),
        name="ffn_up",
    )(a, wg, wu, *cast)


def _glu_res_kernel(a_ref, w1_ref, w2_ref, x_ref, m_ref, o_ref, *, gate_row):
    a = a_ref[...]
    y1 = jnp.dot(a, w1_ref[...], preferred_element_type=F32)
    y2 = jnp.dot(a, w2_ref[...], preferred_element_type=F32)
    gate = m_ref[gate_row:gate_row + 1, :]
    o_ref[...] = x_ref[...] + gate * (y1 * jax.nn.sigmoid(y2))


def _glu_res(a, w1, w2, layer, x, mods, gate_row):
    m, k = a.shape
    n = w1.shape[2]
    tm, tn = _tile(m, 1024), _tile(n, 512)
    return pl.pallas_call(
        functools.partial(_glu_res_kernel, gate_row=gate_row),
        grid=(m // tm, n // tn),
        in_specs=[pl.BlockSpec((tm, k), lambda i, j: (i, 0)),
                  pl.BlockSpec((None, k, tn), lambda i, j: (layer, 0, j)),
                  pl.BlockSpec((None, k, tn), lambda i, j: (layer, 0, j)),
                  pl.BlockSpec((tm, tn), lambda i, j: (i, j)),
                  pl.BlockSpec((mods.shape[0], tn), lambda i, j: (0, j))],
        out_specs=pl.BlockSpec((tm, tn), lambda i, j: (i, j)),
        out_shape=jax.ShapeDtypeStruct((m, n), F32),
        compiler_params=_params("arbitrary", "arbitrary"),
        name="glu_res",
    )(a, w1, w2, x, mods)


def _mm_res_kernel(a_ref, w_ref, x_ref, m_ref, o_ref, acc_ref, *, gate_row, nk, k_valid):
    kk = pl.program_id(2)
    w = w_ref[...]
    tk = w.shape[0]
    row = kk * tk + lax.broadcasted_iota(jnp.int32, (tk, 1), 0)
    w = jnp.where(row < k_valid, w, jnp.zeros_like(w))
    part = jnp.dot(a_ref[...], w, preferred_element_type=F32)

    @pl.when(kk == 0)
    def _():
        acc_ref[...] = part

    @pl.when(kk > 0)
    def _():
        acc_ref[...] += part

    @pl.when(kk == nk - 1)
    def _():
        gate = m_ref[gate_row:gate_row + 1, :]
        o_ref[...] = x_ref[...] + gate * acc_ref[...]


def _mm_res(a, w, layer, x, mods, gate_row, tk):
    m, k = a.shape
    n = w.shape[2]
    tm, tn = _tile(m, 1024), _tile(n, 1024)
    nk = k // tk
    return pl.pallas_call(
        functools.partial(_mm_res_kernel, gate_row=gate_row, nk=nk, k_valid=w.shape[1]),
        grid=(m // tm, n // tn, nk),
        in_specs=[pl.BlockSpec((tm, tk), lambda i, j, kk: (i, kk)),
                  pl.BlockSpec((None, tk, tn), lambda i, j, kk: (layer, kk, j)),
                  pl.BlockSpec((tm, tn), lambda i, j, kk: (i, j)),
                  pl.BlockSpec((mods.shape[0], tn), lambda i, j, kk: (0, j))],
        out_specs=pl.BlockSpec((tm, tn), lambda i, j, kk: (i, j)),
        out_shape=jax.ShapeDtypeStruct((m, n), F32),
        scratch_shapes=[pltpu.VMEM((tm, tn), F32)],
        compiler_params=_params("arbitrary", "arbitrary", "arbitrary"),
        name="ffn_down",
    )(a, w, x, mods)


def _mm2_res_kernel(a1_ref, a2_ref, w1_ref, w2_ref, x_ref, m_ref, o_ref, *, gate_row):
    y = jnp.dot(a1_ref[...], w1_ref[...], preferred_element_type=F32)
    y = y + jnp.dot(a2_ref[...], w2_ref[...], preferred_element_type=F32)
    gate = m_ref[gate_row:gate_row + 1, :]
    o_ref[...] = x_ref[...] + gate * y


def _mm2_res(a1, a2, w, layer, x, mods, gate_row):
    m, k1 = a1.shape
    k2 = a2.shape[1]
    assert k1 == k2 and w.shape[1] == k1 + k2
    n = w.shape[2]
    tm, tn = _tile(m, 1024), _tile(n, 512)
    return pl.pallas_call(
        functools.partial(_mm2_res_kernel, gate_row=gate_row),
        grid=(m // tm, n // tn),
        in_specs=[pl.BlockSpec((tm, k1), lambda i, j: (i, 0)),
                  pl.BlockSpec((tm, k2), lambda i, j: (i, 0)),
                  pl.BlockSpec((None, k1, tn), lambda i, j: (layer, 0, j)),
                  pl.BlockSpec((None, k2, tn), lambda i, j: (layer, 1, j)),
                  pl.BlockSpec((tm, tn), lambda i, j: (i, j)),
                  pl.BlockSpec((mods.shape[0], tn), lambda i, j: (0, j))],
        out_specs=pl.BlockSpec((tm, tn), lambda i, j: (i, j)),
        out_shape=jax.ShapeDtypeStruct((m, n), F32),
        compiler_params=_params("arbitrary", "arbitrary"),
        name="out_proj",
    )(a1, a2, w, w, x, mods)


def _conv_kernel(b_ref, c_ref, xi_ref, ch_ref, xh_ref, w_ref, o_ref):
    i = pl.program_id(0)
    p = c_ref[...].astype(F32) * xi_ref[...].astype(F32)
    halo = ch_ref[...].astype(F32) * xh_ref[...].astype(F32)
    halo = halo * jnp.where(i > 0, 1.0, 0.0)
    nh = halo.shape[0]
    prev1 = halo[nh - 1:nh, :]
    prev2 = halo[nh - 2:nh - 1, :]
    row = lax.broadcasted_iota(jnp.int32, p.shape, 0)
    p1 = jnp.where(row == 0, prev1, pltpu.roll(p, 1, 0))
    p2 = jnp.where(row == 0, prev2, jnp.where(row == 1, prev1, pltpu.roll(p, 2, 0)))
    w = w_ref[...]
    y = w[2:3, :] * p + w[1:2, :] * p1 + w[0:1, :] * p2
    o_ref[...] = (b_ref[...].astype(F32) * y).astype(o_ref.dtype)


def _conv(z, conv_w):
    l = z.shape[0]
    ch = conv_w.shape[1]
    tm, tc = _tile(l, 512), _tile(ch, 512)
    halo = 16
    nc = ch // tc
    rb = tm // halo
    return pl.pallas_call(
        _conv_kernel,
        grid=(l // tm, nc),
        in_specs=[pl.BlockSpec((tm, tc), lambda i, j: (i, j)),
                  pl.BlockSpec((tm, tc), lambda i, j: (i, nc + j)),
                  pl.BlockSpec((tm, tc), lambda i, j: (i, 2 * nc + j)),
                  pl.BlockSpec((halo, tc), lambda i, j: (jnp.maximum(i * rb - 1, 0), nc + j)),
                  pl.BlockSpec((halo, tc), lambda i, j: (jnp.maximum(i * rb - 1, 0), 2 * nc + j)),
                  pl.BlockSpec((CONV_K, tc), lambda i, j: (0, j))],
        out_specs=pl.BlockSpec((tm, tc), lambda i, j: (i, j)),
        out_shape=jax.ShapeDtypeStruct((l, ch), BF16),
        compiler_params=_params("arbitrary", "arbitrary"),
        name="gated_conv",
    )(z, z, z, z, z, conv_w)


def _attn_kernel(lam_ref, g_ref, q_ref, k_ref, v_ref, *rest, tq, tk, lam_init, n_cast):
    cast_in, o_ref, cast_out = rest[:n_cast], rest[n_cast], rest[n_cast + 1:2 * n_cast + 1]
    (m1_ref, l1_ref, a1_ref, m2_ref, l2_ref, a2_ref,
     sa_ref, sb_ref, pa_ref, pb_ref, ala_ref, alb_ref, kn_ref) = rest[2 * n_cast + 1:]
    for src, dst in zip(cast_in, cast_out):
        dst[...] = src[...].astype(dst.dtype)
    qi = pl.program_id(1)
    d = ATTN_HEAD_DIM
    q = q_ref[...].astype(F32) * (d ** -0.5 * math.log2(math.e))
    q1 = q[:, :d].astype(BF16)
    q2 = q[:, d:].astype(BF16)

    def reset():
        m1_ref[...] = jnp.full(m1_ref.shape, MASK_VALUE, F32)
        m2_ref[...] = jnp.full(m2_ref.shape, MASK_VALUE, F32)
        l1_ref[...] = jnp.zeros(l1_ref.shape, F32)
        l2_ref[...] = jnp.zeros(l2_ref.shape, F32)
        a1_ref[...] = jnp.zeros(a1_ref.shape, F32)
        a2_ref[...] = jnp.zeros(a2_ref.shape, F32)

    nt = (((1,), (1,)), ((), ()))
    nch = tk // LANES
    rc = min(tq, 64)
    last = (qi * tq) // tk
    diag_off = qi * tq - last * tk

    def softmax_map(bufs, mi, masked, m_ref, l_ref):
        s_ref, p_ref, al_ref = bufs
        for r0 in range(0, tq, rc):
            rows = slice(r0, r0 + rc)
            s = s_ref[mi, rows, :]
            if masked:
                keep = (lax.broadcasted_iota(jnp.int32, s.shape, 1)
                        <= lax.broadcasted_iota(jnp.int32, s.shape, 0) + (r0 + diag_off))
                s = jnp.where(keep, s, MASK_VALUE)
            cols = [s[:, i * LANES:(i + 1) * LANES] for i in range(nch)]
            mx = functools.reduce(jnp.maximum, cols)
            m_old = m_ref[rows, :]
            m_new = jnp.maximum(m_old, jnp.max(mx, axis=1, keepdims=True))
            alpha = jnp.exp2(m_old - m_new)
            ps = [jnp.exp2(c - m_new) for c in cols]
            l_ref[rows, :] = alpha * l_ref[rows, :] + functools.reduce(jnp.add, ps)
            p_ref[mi, rows, :] = jnp.concatenate([x.astype(BF16) for x in ps], axis=1)
            al_ref[mi, rows, :] = alpha
            m_ref[rows, :] = m_new

    def pv_map(bufs, mi, v, a_ref):
        _, p_ref, al_ref = bufs
        pv = jnp.dot(p_ref[mi], v, preferred_element_type=F32)
        alpha = al_ref[mi]
        a_ref[...] = jnp.concatenate([alpha, alpha], axis=1) * a_ref[...] + pv

    def scores(kb, bufs):
        s_ref = bufs[0]
        ks = pl.multiple_of(kb * tk, tk)
        k = k_ref[pl.ds(ks, tk), :]
        s_ref[0] = lax.dot_general(q1, k[:, :d], nt, preferred_element_type=F32)
        s_ref[1] = lax.dot_general(q2, k[:, d:], nt, preferred_element_type=F32)

    def consume(kb, bufs, masked):
        ks = pl.multiple_of(kb * tk, tk)
        v = v_ref[pl.ds(ks, tk), :]
        softmax_map(bufs, 0, masked, m1_ref, l1_ref)
        pv_map(bufs, 0, v, a1_ref)
        softmax_map(bufs, 1, masked, m2_ref, l2_ref)
        pv_map(bufs, 1, v, a2_ref)

    def exact_path():
        reset()
        buf_a = (sa_ref, pa_ref, ala_ref)
        buf_b = (sb_ref, pb_ref, alb_ref)
        scores(0, buf_a)

        def body(j, carry):
            scores(2 * j + 1, buf_b)
            consume(2 * j, buf_a, False)
            scores(2 * j + 2, buf_a)
            consume(2 * j + 1, buf_b, False)
            return carry

        lax.fori_loop(0, last // 2, body, 0)

        @pl.when(last % 2 == 1)
        def _():
            scores(last, buf_b)
            consume(last - 1, buf_a, False)
            consume(last, buf_b, True)

        @pl.when(last % 2 == 0)
        def _():
            consume(last, buf_a, True)

    @pl.when(qi == 0)
    def _():
        def kbody(t, c):
            kt = k_ref[pl.ds(pl.multiple_of(t * tk, tk), tk), :].astype(F32)
            k1, k2 = kt[:, :d], kt[:, d:]
            n1 = jnp.max(jnp.sum(k1 * k1, axis=1, keepdims=True), axis=0, keepdims=True)
            n2 = jnp.max(jnp.sum(k2 * k2, axis=1, keepdims=True), axis=0, keepdims=True)
            return jnp.maximum(c[0], n1), jnp.maximum(c[1], n2)

        zero11 = jnp.zeros((1, 1), F32)
        n1, n2 = lax.fori_loop(0, k_ref.shape[0] // tk, kbody, (zero11, zero11))
        kn_ref[0] = jnp.broadcast_to(jnp.sqrt(n1), (SUBLANES, LANES))
        kn_ref[1] = jnp.broadcast_to(jnp.sqrt(n2), (SUBLANES, LANES))

    def row_bound(qh, mi):
        qf = qh.astype(F32)
        qn = jnp.sqrt(jnp.sum(qf * qf, axis=1, keepdims=True))
        return jnp.broadcast_to(qn * kn_ref[mi, 0:1, 0:1], (tq, LANES))

    b1 = row_bound(q1, 0)
    b2 = row_bound(q2, 1)

    def fast_map(s, v, bound, masked, l_ref, a_ref):
        if masked:
            keep = (lax.broadcasted_iota(jnp.int32, s.shape, 1)
                    <= lax.broadcasted_iota(jnp.int32, s.shape, 0))
            s = jnp.where(keep, s, MASK_VALUE)
        ps = [jnp.exp2(s[:, i * LANES:(i + 1) * LANES] - bound) for i in range(s.shape[1] // LANES)]
        l_ref[...] += functools.reduce(jnp.add, ps)
        p = jnp.concatenate([x.astype(BF16) for x in ps], axis=1)
        a_ref[...] += jnp.dot(p, v, preferred_element_type=F32)

    def fast_step(ks, width, masked):
        ks = pl.multiple_of(ks, width)
        k = k_ref[pl.ds(ks, width), :]
        v = v_ref[pl.ds(ks, width), :]
        s1 = lax.dot_general(q1, k[:, :d], nt, preferred_element_type=F32)
        s2 = lax.dot_general(q2, k[:, d:], nt, preferred_element_type=F32)
        fast_map(s1, v, b1, masked, l1_ref, a1_ref)
        fast_map(s2, v, b2, masked, l2_ref, a2_ref)

    reset()

    def fast_body(j, carry):
        fast_step(2 * j * tk, tk, False)
        fast_step((2 * j + 1) * tk, tk, False)
        return carry

    lax.fori_loop(0, last // 2, fast_body, 0)

    @pl.when(last % 2 == 1)
    def _():
        fast_step((last - 1) * tk, tk, False)

    @pl.when(diag_off > 0)
    def _():
        fast_step(last * tk, tq, False)

    fast_step(qi * tq, tq, True)

    def row_ok(l_ref):
        return jnp.sum(l_ref[...], axis=1, keepdims=True) >= MIN_NORMALISER
    healthy = jnp.where(jnp.logical_and(row_ok(l1_ref), row_ok(l2_ref)), 1.0, 0.0)
    all_healthy = jnp.min(healthy, axis=0, keepdims=True)[0, 0] > 0.5

    @pl.when(jnp.logical_not(all_healthy))
    def _():
        exact_path()

    lv = lam_ref[...]
    lam = (jnp.exp(jnp.sum(lv[0:1, :] * lv[1:2, :], axis=-1, keepdims=True))
           - jnp.exp(jnp.sum(lv[2:3, :] * lv[3:4, :], axis=-1, keepdims=True)) + lam_init)
    l1 = jnp.sum(l1_ref[...], axis=1, keepdims=True)
    l2 = jnp.sum(l2_ref[...], axis=1, keepdims=True)
    o = a1_ref[...] / l1 - lam * (a2_ref[...] / l2)
    ms = jnp.mean(o * o, axis=-1, keepdims=True)
    o = o * lax.rsqrt(ms + SUBLN_EPS) * g_ref[...] * (1.0 - lam_init)
    o_ref[...] = o.astype(o_ref.dtype)


def _attention(z, lam_vecs, subln_g, lam_init, conv_ch, attn_width, cast=(), cast_layer=0):
    l = z.shape[0]
    hw = 2 * ATTN_HEAD_DIM
    heads = attn_width // hw
    tq, tk = _tile(l, 512), _tile(l, 1024)
    assert tk in (tq, 2 * tq)
    qoff = 3 * conv_ch // hw
    koff = qoff + heads
    voff = koff + heads
    once = pl.Buffered(1)
    grid = (heads, l // tq)
    cast_in, cast_out, cast_shapes = _cast_side_work(cast, cast_layer, grid)
    return pl.pallas_call(
        functools.partial(_attn_kernel, tq=tq, tk=tk, lam_init=lam_init, n_cast=len(cast)),
        grid=grid,
        in_specs=[pl.BlockSpec((4, ATTN_HEAD_DIM), lambda h, i: (0, 0)),
                  pl.BlockSpec((1, hw), lambda h, i: (0, 0)),
                  pl.BlockSpec((tq, hw), lambda h, i: (i, qoff + h)),
                  pl.BlockSpec((l, hw), lambda h, i: (0, koff + h), pipeline_mode=once),
                  pl.BlockSpec((l, hw), lambda h, i: (0, voff + h), pipeline_mode=once)] + cast_in,
        out_specs=[pl.BlockSpec((tq, hw), lambda h, i: (i, h))] + cast_out,
        out_shape=[jax.ShapeDtypeStruct((l, attn_width), BF16)] + cast_shapes,
        scratch_shapes=[pltpu.VMEM((tq, LANES), F32), pltpu.VMEM((tq, LANES), F32), pltpu.VMEM((tq, hw), F32),
                        pltpu.VMEM((tq, LANES), F32), pltpu.VMEM((tq, LANES), F32), pltpu.VMEM((tq, hw), F32),
                        pltpu.VMEM((2, tq, tk), F32), pltpu.VMEM((2, tq, tk), F32),
                        pltpu.VMEM((2, tq, tk), BF16), pltpu.VMEM((2, tq, tk), BF16),
                        pltpu.VMEM((2, tq, LANES), F32), pltpu.VMEM((2, tq, LANES), F32),
                        pltpu.VMEM((2, SUBLANES, LANES), F32)],
        compiler_params=_params("arbitrary", "arbitrary"),
        name="diff_attn",
    )(lam_vecs, subln_g.reshape(1, hw), z, z, z, *cast)


def _s5_prep_kernel(are_ref, aim_ref, ldt_ref, bre_ref, bim_ref, bbre_ref, bbim_ref, pwre_ref, pwim_ref,
                    *, log2_sub):
    lam_re = jnp.minimum(are_ref[...], -1e-4)
    lam_im = aim_ref[...]
    dt = jnp.exp(ldt_ref[...])
    mag = jnp.exp(lam_re * dt)
    ab_re = mag * jnp.cos(lam_im * dt)
    ab_im = mag * jnp.sin(lam_im * dt)
    den = lam_re * lam_re + lam_im * lam_im
    nr, ni = ab_re - 1.0, ab_im
    f_re = (nr * lam_re + ni * lam_im) / den
    f_im = (ni * lam_re - nr * lam_im) / den
    b_re, b_im = bre_ref[...], bim_ref[...]
    bbre_ref[...] = f_re * b_re - f_im * b_im
    bbim_ref[...] = f_re * b_im + f_im * b_re
    pwre_ref[0] = ab_re
    pwim_ref[0] = ab_im
    sr, si = ab_re, ab_im
    for _ in range(log2_sub):
        sr, si = sr * sr - si * si, 2.0 * sr * si
    pr, pi = sr, si
    for k in range(1, SUBLANES + 1):
        pwre_ref[k] = pr
        pwim_ref[k] = pi
        pr, pi = pr * sr - pi * si, pr * si + pi * sr


def _s5_prepare(a_re, a_im, log_dt, b_re, b_im, c_re, c_im, sub_rows):
    g, p = a_re.shape
    h = S5_GROUP
    w = p * h
    rep = lambda t: jnp.repeat(t, h, axis=1)
    tg = _tile(g, 64)
    spec = pl.BlockSpec((tg, w), lambda i: (i, 0))
    npw = SUBLANES + 1
    spec3 = pl.BlockSpec((npw, tg, w), lambda i: (0, i, 0))
    log2_sub = sub_rows.bit_length() - 1
    assert 1 << log2_sub == sub_rows
    bb_re, bb_im, pw_re, pw_im = pl.pallas_call(
        functools.partial(_s5_prep_kernel, log2_sub=log2_sub),
        grid=(g // tg,),
        in_specs=[spec] * 5,
        out_specs=[spec, spec, spec3, spec3],
        out_shape=[jax.ShapeDtypeStruct((g, w), F32)] * 2 + [jax.ShapeDtypeStruct((npw, g, w), F32)] * 2,
        compiler_params=_params("arbitrary"),
        name="s5_prep",
    )(rep(a_re), rep(a_im), jnp.broadcast_to(log_dt[:, None], (g, w)),
      b_re.reshape(g, w), b_im.reshape(g, w))

    nb = g // GROUPS_PER_BLOCK
    eye = jnp.eye(GROUPS_PER_BLOCK, dtype=F32)

    def in_mat(bb):
        t = bb.reshape(nb, GROUPS_PER_BLOCK, p, h).transpose(0, 1, 3, 2)
        t = t[:, :, :, None, :] * eye[None, :, None, :, None]
        return t.reshape(nb, CH_BLOCK, ST_BLOCK)

    def out_mat(c):
        t = c.reshape(nb, GROUPS_PER_BLOCK, h, p).transpose(0, 1, 3, 2)
        t = t[:, :, :, None, :] * eye[None, :, None, :, None]
        return t.reshape(nb, ST_BLOCK, CH_BLOCK)

    b_mat = jnp.concatenate([in_mat(bb_re), in_mat(bb_im)], axis=2).astype(BF16)
    c_mat = jnp.concatenate([out_mat(c_re), -out_mat(c_im)], axis=1).astype(BF16)

    def powers(pw):
        return pw[:, :, ::h].reshape(npw, nb, ST_BLOCK).transpose(1, 0, 2)

    pr, pi = powers(pw_re), powers(pw_im)
    ones = jnp.ones((1, SUBLANES, 1), F32)
    t_idx = jnp.arange(SUBLANES)[None, :, None]
    tabs = [pr[:, 0:1, :] * ones, pi[:, 0:1, :] * ones]
    for dshift in (1, 2, 4):
        keep = (t_idx >= dshift).astype(F32)
        tabs += [pr[:, dshift:dshift + 1, :] * keep, pi[:, dshift:dshift + 1, :] * keep]
    tabs += [pr[:, 1:, :], pi[:, 1:, :]]
    return b_mat, c_mat, jnp.stack(tabs, axis=1)


def _s5_kernel(u_ref, pmt_ref, b_ref, c_ref, t_ref, d_ref, o_ref, st_ref, carry_ref, *, tt):
    ci = pl.program_id(1)
    nlb = ST_BLOCK // LANES
    sub_rows = tt // SUBLANES

    @pl.when(ci == 0)
    def _():
        carry_ref[...] = jnp.zeros(carry_ref.shape, F32)

    def lanes(j):
        return (slice(j * LANES, (j + 1) * LANES),
                slice(ST_BLOCK + j * LANES, ST_BLOCK + (j + 1) * LANES))

    def project_in(h):
        u = u_ref[:, h * CH_BLOCK:(h + 1) * CH_BLOCK]
        st_ref[h] = jnp.dot(u, b_ref[h], preferred_element_type=F32)
        return u

    def scan(h, init, store):
        def step(r, x):
            row = slice(r * SUBLANES, (r + 1) * SUBLANES)
            out = []
            for j in range(nlb):
                lo, hi = lanes(j)
                ar, ai = t_ref[h, 0, :, lo], t_ref[h, 1, :, lo]
                xr, xi = x[2 * j], x[2 * j + 1]
                nr = ar * xr - ai * xi + st_ref[h, row, lo]
                ni = ar * xi + ai * xr + st_ref[h, row, hi]
                if store:
                    st_ref[h, row, lo] = nr
                    st_ref[h, row, hi] = ni
                out += [nr, ni]
            return tuple(out)

        x = init
        for r in range(sub_rows):
            x = step(r, x)
        return x

    def start_states(h, ends):
        first = lax.broadcasted_iota(jnp.int32, (SUBLANES, LANES), 0) == 0
        top = SUBLANES - 1
        init = []
        for j in range(nlb):
            lo, _ = lanes(j)
            yr, yi = ends[2 * j], ends[2 * j + 1]
            for n, dshift in enumerate((1, 2, 4)):
                ar, ai = t_ref[h, 2 + 2 * n, :, lo], t_ref[h, 3 + 2 * n, :, lo]
                rr = pltpu.roll(yr, dshift, 0)
                ri = pltpu.roll(yi, dshift, 0)
                yr, yi = yr + ar * rr - ai * ri, yi + ar * ri + ai * rr
            pr, pi = t_ref[h, 8, :, lo], t_ref[h, 9, :, lo]
            cr, cim = carry_ref[h, 0, :, lo], carry_ref[h, 1, :, lo]
            er = yr + pr * cr - pi * cim
            ei = yi + pr * cim + pi * cr
            init += [jnp.where(first, cr, pltpu.roll(er, 1, 0)),
                     jnp.where(first, cim, pltpu.roll(ei, 1, 0))]
            carry_ref[h, 0, :, lo] = jnp.broadcast_to(er[top:top + 1, :], er.shape)
            carry_ref[h, 1, :, lo] = jnp.broadcast_to(ei[top:top + 1, :], ei.shape)
        return tuple(init)

    def project_out(h, u):
        ch = slice(h * CH_BLOCK, (h + 1) * CH_BLOCK)
        y = jnp.dot(st_ref[h].astype(BF16), c_ref[h], preferred_element_type=F32)
        y = y + d_ref[:, ch] * u.astype(F32)
        g = jax.nn.gelu(y).astype(BF16)
        o_ref[:, ch] = jnp.dot(pmt_ref[...], g, preferred_element_type=F32).astype(o_ref.dtype)

    zero = (jnp.zeros((SUBLANES, LANES), F32),) * (2 * nlb)
    us = [project_in(h) for h in range(BLOCKS_PER_STEP)]
    for h in range(BLOCKS_PER_STEP):
        ends = scan(h, zero, False)
        scan(h, start_states(h, ends), True)
        project_out(h, us[h])


def _s5_tile(l):
    return _tile(l, 512)


def _s5_row_perm(tt):
    rho = jnp.arange(tt)
    pos = (rho % SUBLANES) * (tt // SUBLANES) + rho // SUBLANES
    return (pos[:, None] == jnp.arange(tt)[None, :]).astype(BF16)


def _s5(u, b_mat, c_mat, tabs, d_skip):
    l, d = u.shape
    nb = d // CH_BLOCK
    bps = BLOCKS_PER_STEP
    assert nb % bps == 0
    tt = _s5_tile(l)
    return pl.pallas_call(
        functools.partial(_s5_kernel, tt=tt),
        grid=(nb // bps, l // tt),
        in_specs=[pl.BlockSpec((tt, bps * CH_BLOCK), lambda b, i: (i, b)),
                  pl.BlockSpec((tt, tt), lambda b, i: (0, 0)),
                  pl.BlockSpec((bps, CH_BLOCK, 2 * ST_BLOCK), lambda b, i: (b, 0, 0)),
                  pl.BlockSpec((bps, 2 * ST_BLOCK, CH_BLOCK), lambda b, i: (b, 0, 0)),
                  pl.BlockSpec((bps, 10, SUBLANES, ST_BLOCK), lambda b, i: (b, 0, 0, 0)),
                  pl.BlockSpec((1, bps * CH_BLOCK), lambda b, i: (0, b))],
        out_specs=pl.BlockSpec((tt, bps * CH_BLOCK), lambda b, i: (i, b)),
        out_shape=jax.ShapeDtypeStruct((l, d), BF16),
        scratch_shapes=[pltpu.VMEM((bps, tt, 2 * ST_BLOCK), F32),
                        pltpu.VMEM((bps, 2, SUBLANES, ST_BLOCK), F32)],
        compiler_params=_params("arbitrary", "arbitrary"),
        name="s5_scan",
    )(u, _s5_row_perm(tt).T, b_mat, c_mat, tabs, d_skip.reshape(1, d))


def kernel(x, c, w_ada, b_ada, ada_table, norm_mix, norm_ffn, norm_final, mix_w_in, conv_w, lambda_q1, lambda_k1, lambda_q2, lambda_k2, subln_g, mix_w_out, s5_a_re, s5_a_im, s5_log_dt, s5_b_re, s5_b_im, s5_c_re, s5_c_im, s5_d, glu_w1, glu_w2, ffn_w_gate, ffn_w_up, ffn_w_down):
    bsz, seq, d = x.shape
    assert bsz == 1
    depth = ada_table.shape[0]
    conv_ch = conv_w.shape[2]
    attn_width = mix_w_out.shape[1] - conv_ch
    hidden = ffn_w_gate.shape[2]
    hidden_pad = _round_up(hidden, 512)
    tk_down = hidden_pad // 4
    assert tk_down % LANES == 0 and hidden_pad - hidden < min(512, tk_down)

    w_in = mix_w_in.astype(BF16)
    w_out = mix_w_out.astype(BF16)
    w_glu1 = glu_w1.astype(BF16)
    w_glu2 = glu_w2.astype(BF16)
    ffn_f32 = (ffn_w_gate, ffn_w_up, ffn_w_down)
    w_gate = w_up = w_down = None

    mods = _ada(c, w_ada, b_ada, ada_table)
    xs = x.reshape(seq, d)
    for l in range(depth):
        r = l * N_MOD
        s5_layer = l % 2 == 1
        h = _norm_mod(xs, norm_mix[l], mods, r + 0, r + 1,
                      row_perm=_s5_row_perm(_s5_tile(seq)) if s5_layer else None)
        if not s5_layer:
            e = l // 2
            lam_init = 0.8 - 0.6 * math.exp(-0.3 * l)
            z = _matmul(h, w_in, e)
            y_conv = _conv(z, conv_w[e])
            lam_vecs = jnp.stack([lambda_q1[e], lambda_k1[e], lambda_q2[e], lambda_k2[e]])
            first = ffn_f32 if l == 0 else ()
            y_attn, *w_first = _attention(z, lam_vecs, subln_g[e], lam_init, conv_ch, attn_width,
                                          cast=first, cast_layer=0)
            if first:
                w_gate, w_up, w_down = w_first
            xs = _mm2_res(y_conv, y_attn, w_out, e, xs, mods, r + 2)
        else:
            o = l // 2
            b_mat, c_mat, tabs = _s5_prepare(s5_a_re[o], s5_a_im[o], s5_log_dt[o],
                                             s5_b_re[o], s5_b_im[o], s5_c_re[o], s5_c_im[o],
                                             _s5_tile(seq) // SUBLANES)
            g = _s5(h, b_mat, c_mat, tabs, s5_d[o])
            xs = _glu_res(g, w_glu1, w_glu2, o, xs, mods, r + 2)
        h = _norm_mod(xs, norm_ffn[l], mods, r + 3, r + 4)
        nxt = ffn_f32 if l + 1 < depth else ()
        act, *w_next = _swiglu_up(h, w_gate, w_up, 0, hidden_pad, cast=nxt, cast_layer=l + 1)
        xs = _mm_res(act, w_down, 0, xs, mods, r + 5, tk_down)
        if nxt:
            w_gate, w_up, w_down = w_next
    return _final_norm(xs, norm_final).reshape(bsz, seq, d)
```
